```python
import math
import jax
import jax.numpy as jnp
from jax import lax
import numpy as np

D_MODEL = 2048
BATCH = 2
SEQ = 8192
DEPTH = 4

N_A = DEPTH // 2
N_B = DEPTH - N_A
D_FF = 5632
FFN_RES = 0.5
PLE_DIM = 256
EPS = 1e-6
GLA_HEADS = 4
GLA_QK = D_MODEL // 2
GLA_V = D_MODEL
GLA_DK = GLA_QK // GLA_HEADS
GLA_DV = GLA_V // GLA_HEADS
GLA_RANK = 16
GLA_GATE_NORM = 16.0
GLA_CHUNK = 64
GLA_IN = 2 * GLA_QK + 2 * GLA_V + GLA_RANK
HEAD_DIM = 64
N_Q_HEADS = D_MODEL // HEAD_DIM
N_KV_HEADS = 4
GROUP = N_Q_HEADS // N_KV_HEADS
Q_W = N_Q_HEADS * HEAD_DIM
KV_W = N_KV_HEADS * HEAD_DIM
WINDOW = 128
BLOCK = WINDOW
N_BUCKETS = 32
MAX_DISTANCE = WINDOW

kernel_name = "yoco_gla_swa_sink_macaron_hybrid"


def rms_norm(x, g):
    xf = x.astype(jnp.float32)
    y = xf * lax.rsqrt(jnp.mean(xf * xf, axis=-1, keepdims=True) + EPS)
    return (y * g.astype(jnp.float32)).astype(x.dtype)


def swiglu(h, wg, wu, wd):
    return (jax.nn.silu(h @ wg) * (h @ wu)) @ wd


def gla_mixer(h, w_in, w_a2, b_a, g_o, w_o):
    bsz, L, _ = h.shape
    H, dk, dv, C = GLA_HEADS, GLA_DK, GLA_DV, GLA_CHUNK
    nc = L // C
    f32 = jnp.float32
    proj = h @ w_in
    q, k, v, r, a1 = jnp.split(proj, [GLA_QK, 2 * GLA_QK, 2 * GLA_QK + GLA_V, 2 * GLA_QK + 2 * GLA_V], axis=-1)
    log_a = jax.nn.log_sigmoid((a1 @ w_a2 + b_a).astype(f32)) / GLA_GATE_NORM

    def to_chunks(t, d):
        return t.reshape(bsz, nc, C, H, d).transpose(0, 3, 1, 2, 4)

    q = to_chunks(q, dk).astype(f32) * (dk ** -0.5)
    k = to_chunks(k, dk).astype(f32)
    v = to_chunks(v, dv).astype(f32)
    b = jnp.cumsum(to_chunks(log_a, dk), axis=3)
    b_last = b[:, :, :, -1:, :]
    q_dec = q * jnp.exp(b)
    k_inv = k * jnp.exp(-b)
    k_dec = k * jnp.exp(b_last - b)
    causal = jnp.tril(jnp.ones((C, C), dtype=bool))
    attn = jnp.where(causal, jnp.einsum('bhnik,bhnjk->bhnij', q_dec, k_inv), 0.0)
    o_intra = jnp.einsum('bhnij,bhnjv->bhniv', attn, v)

    def step(S, xs):
        qd, kd, vv, dec = xs
        o = jnp.einsum('bhck,bhkv->bhcv', qd, S)
        S = dec[..., None] * S + jnp.einsum('bhck,bhcv->bhkv', kd, vv)
        return S, o

    S0 = jnp.zeros((bsz, H, dk, dv), f32)
    xs = (jnp.moveaxis(q_dec, 2, 0), jnp.moveaxis(k_dec, 2, 0), jnp.moveaxis(v, 2, 0),
          jnp.moveaxis(jnp.exp(b_last[:, :, :, 0, :]), 2, 0))
    _, o_inter = lax.scan(step, S0, xs)
    o = o_intra + jnp.moveaxis(o_inter, 0, 2)
    o = o.transpose(0, 2, 3, 1, 4).reshape(bsz, L, H, dv)
    o = rms_norm(o, g_o).astype(h.dtype).reshape(bsz, L, H * dv)
    return (o * jax.nn.silu(r)) @ w_o


def shared_kv(h, g_kv, w_kv, g_k):
    bsz, L, _ = h.shape
    nb = L // BLOCK
    kv = rms_norm(h, g_kv) @ w_kv
    k, v = jnp.split(kv, [KV_W], axis=-1)
    k = rms_norm(k.reshape(bsz, L, N_KV_HEADS, HEAD_DIM), g_k)
    v = v.reshape(bsz, L, N_KV_HEADS, HEAD_DIM)

    def banded(t):
        tb = t.reshape(bsz, nb, BLOCK, N_KV_HEADS, HEAD_DIM)
        prev = jnp.pad(tb[:, :-1], ((0, 0), (1, 0), (0, 0), (0, 0), (0, 0)))
        return jnp.concatenate([prev, tb], axis=2)

    return banded(k), banded(v)


def t5_bucket(d):
    max_exact = N_BUCKETS // 2
    is_small = d < max_exact
    df = jnp.maximum(d, 1).astype(jnp.float32)
    large = max_exact + (jnp.log(df / max_exact) / math.log(MAX_DISTANCE / max_exact)
                         * (N_BUCKETS - max_exact)).astype(jnp.int32)
    large = jnp.minimum(large, N_BUCKETS - 1)
    return jnp.where(is_small, d, large)


def swa_mixer(h, k_band, v_band, bias, mask, w_q, g_q, sinks, w_o):
    bsz, L, _ = h.shape
    nb = L // BLOCK
    q = rms_norm((h @ w_q).reshape(bsz, L, N_KV_HEADS, GROUP, HEAD_DIM), g_q)
    q = q.reshape(bsz, nb, BLOCK, N_KV_HEADS, GROUP, HEAD_DIM)
    s = jnp.einsum('bnqkgd,bnskd->bkgnqs', q, k_band,
                   preferred_element_type=jnp.float32) * (HEAD_DIM ** -0.5)
    s = jnp.where(mask, s + bias, -jnp.inf)
    sink = sinks.astype(jnp.float32).reshape(N_KV_HEADS, GROUP, 1, 1, 1)
    m = jnp.maximum(jnp.max(s, axis=-1, keepdims=True), sink)
    pr = jnp.exp(s - m)
    pr = pr / (jnp.sum(pr, axis=-1, keepdims=True) + jnp.exp(sink - m))
    o = jnp.einsum('bkgnqs,bnskd->bnqkgd', pr.astype(v_band.dtype), v_band)
    return o.reshape(bsz, L, Q_W) @ w_o


def setup_inputs(seed: int = 0) -> dict:
    key = jax.random.key(seed)
    ks = jax.random.split(key, 21)
    f32 = jnp.float32

    def nrm(k, shape, scale):
        return jax.random.normal(k, shape, f32) * scale

    def gain(k, shape):
        return 1.0 + 0.02 * jax.random.normal(k, shape, f32)

    return {
        'x': nrm(ks[0], (BATCH, SEQ, D_MODEL), 1.0),
        'p': nrm(ks[1], (DEPTH, BATCH, SEQ, PLE_DIM), 1.0),
        'norm_g': gain(ks[2], (DEPTH, 4, D_MODEL)),
        'ffn_wg': nrm(ks[3], (DEPTH, 2, D_MODEL, D_FF), D_MODEL ** -0.5),
        'ffn_wu': nrm(ks[4], (DEPTH, 2, D_MODEL, D_FF), D_MODEL ** -0.5),
        'ffn_wd': nrm(ks[5], (DEPTH, 2, D_FF, D_MODEL), D_FF ** -0.5),
        'ple_w': nrm(ks[6], (DEPTH, PLE_DIM, D_MODEL), PLE_DIM ** -0.5),
        'ple_gate_w': nrm(ks[7], (DEPTH, D_MODEL, D_MODEL), D_MODEL ** -0.5),
        'gla_w_in': nrm(ks[8], (N_A, D_MODEL, GLA_IN), D_MODEL ** -0.5),
        'gla_w_a2': nrm(ks[9], (N_A, GLA_RANK, GLA_QK), GLA_RANK ** -0.5),
        'gla_b_a': nrm(ks[10], (N_A, GLA_QK), 0.1),
        'gla_norm_g': gain(ks[11], (N_A, GLA_DV)),
        'gla_w_o': nrm(ks[12], (N_A, GLA_V, D_MODEL), GLA_V ** -0.5),
        'kv_norm_g': gain(ks[13], (D_MODEL,)),
        'w_kv': nrm(ks[14], (D_MODEL, 2 * KV_W), D_MODEL ** -0.5),
        'k_norm_g': gain(ks[15], (HEAD_DIM,)),
        'rel_bias': nrm(ks[16], (N_BUCKETS, N_Q_HEADS), 0.5),
        'swa_w_q': nrm(ks[17], (N_B, D_MODEL, Q_W), D_MODEL ** -0.5),
        'q_norm_g': gain(ks[18], (N_B, HEAD_DIM)),
        'sinks': nrm(ks[19], (N_B, N_Q_HEADS), 0.5),
        'swa_w_o': nrm(ks[20], (N_B, Q_W, D_MODEL), Q_W ** -0.5),
    }


def reference(x, p, norm_g, ffn_wg, ffn_wu, ffn_wd, ple_w, ple_gate_w,
              gla_w_in, gla_w_a2, gla_b_a, gla_norm_g, gla_w_o,
              kv_norm_g, w_kv, k_norm_g, rel_bias, swa_w_q, q_norm_g, sinks, swa_w_o):
    L = x.shape[1]
    nb = L // BLOCK
    qi = jnp.arange(BLOCK)[:, None]
    sj = jnp.arange(2 * BLOCK)[None, :]
    dist = qi + BLOCK - sj
    in_win = (dist >= 0) & (dist < WINDOW)
    kpos = jnp.arange(nb)[:, None, None] * BLOCK - BLOCK + sj[None]
    mask = in_win[None] & (kpos >= 0)
    bias = rel_bias[t5_bucket(jnp.maximum(dist, 0))].astype(jnp.float32)
    bias = bias.transpose(2, 0, 1).reshape(N_KV_HEADS, GROUP, 1, BLOCK, 2 * BLOCK)

    h = x
    k_band = None
    v_band = None
    for i in range(DEPTH):
        if i == N_A:
            k_band, v_band = shared_kv(h, kv_norm_g, w_kv, k_norm_g)
        h = h + FFN_RES * swiglu(rms_norm(h, norm_g[i, 0]), ffn_wg[i, 0], ffn_wu[i, 0], ffn_wd[i, 0])
        hn = rms_norm(h, norm_g[i, 1])
        if i < N_A:
            h = h + gla_mixer(hn, gla_w_in[i], gla_w_a2[i], gla_b_a[i], gla_norm_g[i], gla_w_o[i])
        else:
            j = i - N_A
            h = h + swa_mixer(hn, k_band, v_band, bias, mask, swa_w_q[j], q_norm_g[j], sinks[j], swa_w_o[j])
        h = h + FFN_RES * swiglu(rms_norm(h, norm_g[i, 2]), ffn_wg[i, 1], ffn_wu[i, 1], ffn_wd[i, 1])
        gate = jax.nn.sigmoid(rms_norm(h, norm_g[i, 3]) @ ple_gate_w[i])
        h = h + gate * (p[i] @ ple_w[i])
    return h
```

```python
import math

import jax
import jax.numpy as jnp
from jax import lax
from jax.experimental import pallas as pl
from jax.experimental.pallas import tpu as pltpu

F32 = jnp.float32
BF16 = jnp.bfloat16

D_MODEL = 2048
D_FF = 5632
FFN_RES = 0.5
PLE_DIM = 256
EPS = 1e-6
GLA_HEADS = 4
GLA_QK = 1024
GLA_V = 2048
GLA_DK = 256
GLA_DV = 512
GLA_RANK = 16
GLA_GATE_NORM = 16.0
GLA_CHUNK = 64
GLA_MAIN = 2 * GLA_QK + 2 * GLA_V
HEAD_DIM = 64
N_Q_HEADS = 32
N_KV_HEADS = 4
GROUP = 8
KV_W = N_KV_HEADS * HEAD_DIM
WINDOW = 128
BLOCK = WINDOW
N_BUCKETS = 32
MAX_DISTANCE = WINDOW

V7X_LANES = 128
V7X_VMEM_BYTES = 64 * 1024 * 1024
VMEM_LIMIT = V7X_VMEM_BYTES - 8 * 1024 * 1024

TM = 512
TF = 512
TN_PROJ = 1024
T_GLA = 512


def _params(*sem):
    return pltpu.CompilerParams(dimension_semantics=sem, vmem_limit_bytes=VMEM_LIMIT)


def _rms(x, g):
    return x * lax.rsqrt(jnp.mean(x * x, axis=-1, keepdims=True) + EPS) * g


def _dot(a, b):
    return jnp.dot(a, b, preferred_element_type=F32)


def _dot_nt(a, b):
    return lax.dot_general(a, b, (((1,), (1,)), ((), ())), preferred_element_type=F32)


def _dot_tn(a, b):
    return lax.dot_general(a, b, (((0,), (0,)), ((), ())), preferred_element_type=F32)


def _ffn_kernel(h_ref, g_ref, wg_ref, wu_ref, wd_ref, o_ref, xn_ref):
    j = pl.program_id(1)

    @pl.when(j == 0)
    def _():
        x = h_ref[...]
        xn_ref[...] = _rms(x, g_ref[...]).astype(BF16)
        o_ref[...] = x

    xn = xn_ref[...]
    gate = _dot(xn, wg_ref[...])
    up = _dot(xn, wu_ref[...])
    act = (gate * jax.nn.sigmoid(gate) * up * FFN_RES).astype(BF16)
    o_ref[...] += _dot(act, wd_ref[...])


def _ffn(h, g, wg, wu, wd):
    m = h.shape[0]
    return pl.pallas_call(
        _ffn_kernel,
        grid=(m // TM, D_FF // TF),
        in_specs=[
            pl.BlockSpec((TM, D_MODEL), lambda i, j: (i, 0)),
            pl.BlockSpec((1, D_MODEL), lambda i, j: (0, 0)),
            pl.BlockSpec((D_MODEL, TF), lambda i, j: (0, j)),
            pl.BlockSpec((D_MODEL, TF), lambda i, j: (0, j)),
            pl.BlockSpec((TF, D_MODEL), lambda i, j: (j, 0)),
        ],
        out_specs=pl.BlockSpec((TM, D_MODEL), lambda i, j: (i, 0)),
        out_shape=jax.ShapeDtypeStruct((m, D_MODEL), F32),
        scratch_shapes=[pltpu.VMEM((TM, D_MODEL), BF16)],
        compiler_params=_params("parallel", "arbitrary"),
        name="ffn",
    )(h, g, wg, wu, wd)


def _gla_proj_kernel(h_ref, g_ref, w_ref, wa1_ref, wa2_ref, ba_ref, proj_ref, la_ref, xn_ref):
    j = pl.program_id(1)

    @pl.when(j == 0)
    def _():
        xn = _rms(h_ref[...], g_ref[...]).astype(BF16)
        xn_ref[...] = xn
        a1 = _dot(xn, wa1_ref[...])
        z = _dot(a1.astype(BF16), wa2_ref[...]) + ba_ref[...]
        log_sig = jnp.minimum(z, 0.0) - jnp.log1p(jnp.exp(-jnp.abs(z)))
        la_ref[...] = log_sig * (1.0 / GLA_GATE_NORM)

    proj_ref[...] = _dot(xn_ref[...], w_ref[...])


def _gla_proj(h, g, w_main, w_a1, w_a2, b_a):
    m = h.shape[0]
    return pl.pallas_call(
        _gla_proj_kernel,
        grid=(m // TM, GLA_MAIN // TN_PROJ),
        in_specs=[
            pl.BlockSpec((TM, D_MODEL), lambda i, j: (i, 0)),
            pl.BlockSpec((1, D_MODEL), lambda i, j: (0, 0)),
            pl.BlockSpec((D_MODEL, TN_PROJ), lambda i, j: (0, j)),
            pl.BlockSpec((D_MODEL, V7X_LANES), lambda i, j: (0, 0)),
            pl.BlockSpec((V7X_LANES, GLA_QK), lambda i, j: (0, 0)),
            pl.BlockSpec((1, GLA_QK), lambda i, j: (0, 0)),
        ],
        out_specs=[
            pl.BlockSpec((TM, TN_PROJ), lambda i, j: (i, j)),
            pl.BlockSpec((TM, GLA_QK), lambda i, j: (i, 0)),
        ],
        out_shape=[
            jax.ShapeDtypeStruct((m, GLA_MAIN), F32),
            jax.ShapeDtypeStruct((m, GLA_QK), F32),
        ],
        scratch_shapes=[pltpu.VMEM((TM, D_MODEL), BF16)],
        compiler_params=_params("parallel", "arbitrary"),
        name="gla_proj",
    )(h, g, w_main, w_a1, w_a2, b_a)


def _gla_rec_kernel(q_ref, k_ref, v_ref, r_ref, la_ref, go_ref, o_ref, st_ref):
    c_sz = GLA_CHUNK

    @pl.when(pl.program_id(2) == 0)
    def _():
        st_ref[...] = jnp.zeros_like(st_ref)

    row = lax.broadcasted_iota(jnp.int32, (c_sz, c_sz), 0)
    col = lax.broadcasted_iota(jnp.int32, (c_sz, c_sz), 1)
    causal = col <= row
    tril = jnp.where(causal, 1.0, 0.0).astype(BF16)
    go = go_ref[...]

    def chunk(c, carry):
        sl = pl.ds(pl.multiple_of(c * c_sz, c_sz), c_sz)
        la = la_ref[sl, :]
        la1 = la.astype(BF16)
        rem = la - la1.astype(F32)
        la2 = rem.astype(BF16)
        la3 = (rem - la2.astype(F32)).astype(BF16)
        b = _dot(tril, la1) + _dot(tril, la2) + _dot(tril, la3)
        b_last = b[c_sz - 1:c_sz, :]
        q = q_ref[sl, :] * (GLA_DK ** -0.5)
        k = k_ref[sl, :]
        v = v_ref[sl, :].astype(BF16)
        q_dec = (q * jnp.exp(b)).astype(BF16)
        k_inv = (k * jnp.exp(-b)).astype(BF16)
        k_dec = (k * jnp.exp(b_last - b)).astype(BF16)
        attn = jnp.where(causal, _dot_nt(q_dec, k_inv), 0.0)
        st = st_ref[...]
        o = _dot(attn.astype(BF16), v) + _dot_nt(q_dec, st.astype(BF16))
        st_ref[...] = st * jnp.exp(b_last) + _dot_tn(v, k_dec)
        r = r_ref[sl, :]
        o_ref[sl, :] = (_rms(o, go) * (r * jax.nn.sigmoid(r))).astype(BF16)
        return carry

    lax.fori_loop(0, T_GLA // c_sz, chunk, 0)


def _gla_rec(proj, log_a, g_o):
    bsz, seq, _ = proj.shape
    k_off = GLA_QK // GLA_DK
    v_off = 2 * GLA_QK // GLA_DV
    r_off = (2 * GLA_QK + GLA_V) // GLA_DV
    return pl.pallas_call(
        _gla_rec_kernel,
        grid=(bsz, GLA_HEADS, seq // T_GLA),
        in_specs=[
            pl.BlockSpec((None, T_GLA, GLA_DK), lambda b, h, t: (b, t, h)),
            pl.BlockSpec((None, T_GLA, GLA_DK), lambda b, h, t: (b, t, k_off + h)),
            pl.BlockSpec((None, T_GLA, GLA_DV), lambda b, h, t: (b, t, v_off + h)),
            pl.BlockSpec((None, T_GLA, GLA_DV), lambda b, h, t: (b, t, r_off + h)),
            pl.BlockSpec((None, T_GLA, GLA_DK), lambda b, h, t: (b, t, h)),
            pl.BlockSpec((1, GLA_DV), lambda b, h, t: (0, 0)),
        ],
        out_specs=pl.BlockSpec((None, T_GLA, GLA_DV), lambda b, h, t: (b, t, h)),
        out_shape=jax.ShapeDtypeStruct((bsz, seq, GLA_V), BF16),
        scratch_shapes=[pltpu.VMEM((GLA_DV, GLA_DK), F32)],
        compiler_params=_params("parallel", "parallel", "arbitrary"),
        name="gla_rec",
    )(proj, proj, proj, proj, log_a, g_o)


def _matmul_res_kernel(a_ref, w_ref, res_ref, o_ref):
    o_ref[...] = res_ref[...] + _dot(a_ref[...], w_ref[...])


def _matmul_res(a, w, res):
    m, kdim = a.shape
    n = w.shape[1]
    return pl.pallas_call(
        _matmul_res_kernel,
        grid=(m // TM,),
        in_specs=[
            pl.BlockSpec((TM, kdim), lambda i: (i, 0)),
            pl.BlockSpec((kdim, n), lambda i: (0, 0)),
            pl.BlockSpec((TM, n), lambda i: (i, 0)),
        ],
        out_specs=pl.BlockSpec((TM, n), lambda i: (i, 0)),
        out_shape=jax.ShapeDtypeStruct((m, n), F32),
        compiler_params=_params("parallel"),
        name="matmul_res",
    )(a, w, res)


def _normed_matmul_kernel(h_ref, g_ref, w_ref, o_ref):
    o_ref[...] = _dot(_rms(h_ref[...], g_ref[...]).astype(BF16), w_ref[...])


def _normed_matmul(h, g, w):
    m = h.shape[0]
    n = w.shape[1]
    return pl.pallas_call(
        _normed_matmul_kernel,
        grid=(m // TM,),
        in_specs=[
            pl.BlockSpec((TM, D_MODEL), lambda i: (i, 0)),
            pl.BlockSpec((1, D_MODEL), lambda i: (0, 0)),
            pl.BlockSpec((D_MODEL, n), lambda i: (0, 0)),
        ],
        out_specs=pl.BlockSpec((TM, n), lambda i: (i, 0)),
        out_shape=jax.ShapeDtypeStruct((m, n), F32),
        compiler_params=_params("parallel"),
        name="normed_matmul",
    )(h, g, w)


def _kv_kernel(h_ref, g_ref, w_ref, gk_ref, k_ref, v_ref):
    kv = _dot(_rms(h_ref[...], g_ref[...]).astype(BF16), w_ref[...])
    gk = gk_ref[...]
    heads = [_rms(kv[:, i * HEAD_DIM:(i + 1) * HEAD_DIM], gk) for i in range(N_KV_HEADS)]
    k_ref[...] = jnp.concatenate(heads, axis=1).astype(BF16)
    v_ref[...] = kv[:, KV_W:].astype(BF16)


def _shared_kv(h, g, w, gk):
    m = h.shape[0]
    return pl.pallas_call(
        _kv_kernel,
        grid=(m // TM,),
        in_specs=[
            pl.BlockSpec((TM, D_MODEL), lambda i: (i, 0)),
            pl.BlockSpec((1, D_MODEL), lambda i: (0, 0)),
            pl.BlockSpec((D_MODEL, 2 * KV_W), lambda i: (0, 0)),
            pl.BlockSpec((1, HEAD_DIM), lambda i: (0, 0)),
        ],
        out_specs=[
            pl.BlockSpec((TM, KV_W), lambda i: (i, 0)),
            pl.BlockSpec((TM, KV_W), lambda i: (i, 0)),
        ],
        out_shape=[
            jax.ShapeDtypeStruct((m, KV_W), BF16),
            jax.ShapeDtypeStruct((m, KV_W), BF16),
        ],
        compiler_params=_params("parallel"),
        name="shared_kv",
    )(h, g, w, gk)


def _bias_kernel(rb_ref, bucket_ref, o_ref):
    h = pl.program_id(0)
    bucket = bucket_ref[...]
    acc = jnp.zeros(bucket.shape, F32)
    for b in range(N_BUCKETS):
        acc = jnp.where(bucket == b, rb_ref[b, h], acc)
    o_ref[...] = acc


def _bias_table(rel_bias, bucket):
    return pl.pallas_call(
        _bias_kernel,
        grid=(N_Q_HEADS,),
        in_specs=[
            pl.BlockSpec(memory_space=pltpu.SMEM),
            pl.BlockSpec((BLOCK, 2 * BLOCK), lambda h: (0, 0)),
        ],
        out_specs=pl.BlockSpec((None, BLOCK, 2 * BLOCK), lambda h: (h, 0, 0)),
        out_shape=jax.ShapeDtypeStruct((N_Q_HEADS, BLOCK, 2 * BLOCK), F32),
        compiler_params=_params("parallel"),
        name="bias_table",
    )(rel_bias, bucket)


def _swa_kernel(sinks_ref, q_ref, kp_ref, kc_ref, vp_ref, vc_ref, bias_ref, gq_ref, o_ref):
    n = pl.program_id(1)
    kb = jnp.concatenate([kp_ref[...], kc_ref[...]], axis=0)
    vb = jnp.concatenate([vp_ref[...], vc_ref[...]], axis=0)
    qi = lax.broadcasted_iota(jnp.int32, (BLOCK, 2 * BLOCK), 0)
    sj = lax.broadcasted_iota(jnp.int32, (BLOCK, 2 * BLOCK), 1)
    dist = qi + BLOCK - sj
    kpos = n * BLOCK - BLOCK + sj
    mask = (dist >= 0) & (dist < WINDOW) & (kpos >= 0)
    gq = gq_ref[...]
    outs = []
    for h in range(N_Q_HEADS):
        kh = h // GROUP
        lo, hi = kh * HEAD_DIM, (kh + 1) * HEAD_DIM
        qn = _rms(q_ref[:, h * HEAD_DIM:(h + 1) * HEAD_DIM], gq).astype(BF16)
        s = _dot_nt(qn, kb[:, lo:hi]) * (HEAD_DIM ** -0.5)
        s = jnp.where(mask, s + bias_ref[h], -jnp.inf)
        sink = sinks_ref[h]
        mx = jnp.maximum(jnp.max(s, axis=-1, keepdims=True), sink)
        p = jnp.exp(s - mx)
        denom = jnp.sum(p, axis=-1, keepdims=True) + jnp.exp(sink - mx)
        outs.append(_dot(p.astype(BF16), vb[:, lo:hi]) / denom)
    o_ref[...] = jnp.concatenate(outs, axis=1).astype(BF16)


def _swa_attn(q, k, v, bias, sinks, gq):
    bsz, seq, _ = q.shape
    nb = seq // BLOCK
    prev = lambda b, n: (b, jnp.maximum(n - 1, 0), 0)
    cur = lambda b, n: (b, n, 0)
    return pl.pallas_call(
        _swa_kernel,
        grid=(bsz, nb),
        in_specs=[
            pl.BlockSpec(memory_space=pltpu.SMEM),
            pl.BlockSpec((None, BLOCK, D_MODEL), cur),
            pl.BlockSpec((None, BLOCK, KV_W), prev),
            pl.BlockSpec((None, BLOCK, KV_W), cur),
            pl.BlockSpec((None, BLOCK, KV_W), prev),
            pl.BlockSpec((None, BLOCK, KV_W), cur),
            pl.BlockSpec((N_Q_HEADS, BLOCK, 2 * BLOCK), lambda b, n: (0, 0, 0)),
            pl.BlockSpec((1, HEAD_DIM), lambda b, n: (0, 0)),
        ],
        out_specs=pl.BlockSpec((None, BLOCK, D_MODEL), cur),
        out_shape=jax.ShapeDtypeStruct((bsz, seq, D_MODEL), BF16),
        compiler_params=_params("parallel", "parallel"),
        name="swa_attn",
    )(sinks, q, k, k, v, v, bias, gq)


def _ple_kernel(h_ref, g_ref, wgate_ref, p_ref, wple_ref, o_ref):
    x = h_ref[...]
    gate = jax.nn.sigmoid(_dot(_rms(x, g_ref[...]).astype(BF16), wgate_ref[...]))
    o_ref[...] = x + gate * _dot(p_ref[...].astype(BF16), wple_ref[...])


def _ple(h, g, w_gate, p, w_ple):
    m = h.shape[0]
    return pl.pallas_call(
        _ple_kernel,
        grid=(m // TM,),
        in_specs=[
            pl.BlockSpec((TM, D_MODEL), lambda i: (i, 0)),
            pl.BlockSpec((1, D_MODEL), lambda i: (0, 0)),
            pl.BlockSpec((D_MODEL, D_MODEL), lambda i: (0, 0)),
            pl.BlockSpec((TM, PLE_DIM), lambda i: (i, 0)),
            pl.BlockSpec((PLE_DIM, D_MODEL), lambda i: (0, 0)),
        ],
        out_specs=pl.BlockSpec((TM, D_MODEL), lambda i: (i, 0)),
        out_shape=jax.ShapeDtypeStruct((m, D_MODEL), F32),
        compiler_params=_params("parallel"),
        name="ple",
    )(h, g, w_gate, p, w_ple)


def _t5_bucket(d):
    max_exact = N_BUCKETS // 2
    is_small = d < max_exact
    df = jnp.maximum(d, 1).astype(F32)
    large = max_exact + (jnp.log(df / max_exact) / math.log(MAX_DISTANCE / max_exact)
                         * (N_BUCKETS - max_exact)).astype(jnp.int32)
    large = jnp.minimum(large, N_BUCKETS - 1)
    return jnp.where(is_small, d, large)


def kernel(x, p, norm_g, ffn_wg, ffn_wu, ffn_wd, ple_w, ple_gate_w, gla_w_in, gla_w_a2, gla_b_a,
           gla_norm_g, gla_w_o, kv_norm_g, w_kv, k_norm_g, rel_bias, swa_w_q, q_norm_g, sinks, swa_w_o):
    bsz, seq, d = x.shape
    depth = norm_g.shape[0]
    n_a = gla_w_in.shape[0]
    m = bsz * seq
    row = lambda v: v.reshape(1, -1)

    qi = jnp.arange(BLOCK)[:, None]
    sj = jnp.arange(2 * BLOCK)[None, :]
    bucket = _t5_bucket(jnp.maximum(qi + BLOCK - sj, 0)).astype(jnp.int32)
    bias = _bias_table(rel_bias, bucket)

    h = x.reshape(m, d)
    k_sh = v_sh = None
    for i in range(depth):
        if i == n_a:
            k_sh, v_sh = _shared_kv(h, row(kv_norm_g), w_kv.astype(BF16), row(k_norm_g))
            k_sh = k_sh.reshape(bsz, seq, KV_W)
            v_sh = v_sh.reshape(bsz, seq, KV_W)
        h = _ffn(h, row(norm_g[i, 0]), ffn_wg[i, 0].astype(BF16), ffn_wu[i, 0].astype(BF16),
                 ffn_wd[i, 0].astype(BF16))
        if i < n_a:
            w_in = gla_w_in[i]
            w_a1 = jnp.pad(w_in[:, GLA_MAIN:], ((0, 0), (0, V7X_LANES - GLA_RANK))).astype(BF16)
            w_a2 = jnp.pad(gla_w_a2[i], ((0, V7X_LANES - GLA_RANK), (0, 0))).astype(BF16)
            proj, log_a = _gla_proj(h, row(norm_g[i, 1]), w_in[:, :GLA_MAIN].astype(BF16), w_a1, w_a2,
                                    row(gla_b_a[i]))
            og = _gla_rec(proj.reshape(bsz, seq, GLA_MAIN), log_a.reshape(bsz, seq, GLA_QK),
                          row(gla_norm_g[i]))
            h = _matmul_res(og.reshape(m, GLA_V), gla_w_o[i].astype(BF16), h)
        else:
            j = i - n_a
            q = _normed_matmul(h, row(norm_g[i, 1]), swa_w_q[j].astype(BF16))
            o = _swa_attn(q.reshape(bsz, seq, d), k_sh, v_sh, bias, sinks[j], row(q_norm_g[j]))
            h = _matmul_res(o.reshape(m, d), swa_w_o[j].astype(BF16), h)
        h = _ffn(h, row(norm_g[i, 2]), ffn_wg[i, 1].astype(BF16), ffn_wu[i, 1].astype(BF16),
                 ffn_wd[i, 1].astype(BF16))
        h = _ple(h, row(norm_g[i, 3]), ple_gate_w[i].astype(BF16), p[i].reshape(m, PLE_DIM),
                 ple_w[i].astype(BF16))
    return h.reshape(bsz, seq, d)
```

```python
import math

import jax
import jax.numpy as jnp
from jax import lax
from jax.experimental import pallas as pl
from jax.experimental.pallas import tpu as pltpu

F32 = jnp.float32
BF16 = jnp.bfloat16

D_MODEL = 2048
D_FF = 5632
FFN_RES = 0.5
PLE_DIM = 256
EPS = 1e-6
GLA_HEADS = 4
GLA_QK = 1024
GLA_V = 2048
GLA_DK = 256
GLA_DV = 512
GLA_RANK = 16
GLA_GATE_NORM = 16.0
GLA_CHUNK = 64
GLA_MAIN = 2 * GLA_QK + 2 * GLA_V
HEAD_DIM = 64
N_Q_HEADS = 32
N_KV_HEADS = 4
GROUP = 8
KV_W = N_KV_HEADS * HEAD_DIM
WINDOW = 128
BLOCK = WINDOW
N_BUCKETS = 32
MAX_DISTANCE = WINDOW

V7X_LANES = 128
V7X_VMEM_BYTES = 64 * 1024 * 1024
VMEM_LIMIT = V7X_VMEM_BYTES - 8 * 1024 * 1024

PAIR_W = 2 * HEAD_DIM
assert PAIR_W == V7X_LANES
N_PAIRS = N_Q_HEADS // 2
GROUP_PAIRS = GROUP // 2
GROUP_ROWS = GROUP * BLOCK

TM = 512
TF = 512
TM_PROJ = 1024
TN_PROJ = 1024
T_GLA = 512


def _params(*sem):
    return pltpu.CompilerParams(dimension_semantics=sem, vmem_limit_bytes=VMEM_LIMIT)


def _rms(x, g):
    return x * lax.rsqrt(jnp.mean(x * x, axis=-1, keepdims=True) + EPS) * g


def _dot(a, b):
    return jnp.dot(a, b, preferred_element_type=F32)


def _dot_nt(a, b):
    return lax.dot_general(a, b, (((1,), (1,)), ((), ())), preferred_element_type=F32)


def _dot_tn(a, b):
    return lax.dot_general(a, b, (((0,), (0,)), ((), ())), preferred_element_type=F32)


def _low_half(shape):
    return lax.broadcasted_iota(jnp.int32, shape, len(shape) - 1) < HEAD_DIM


def _ffn_kernel(h_ref, g_ref, wg_ref, wu_ref, wd_ref, o_ref, xn_ref):
    j = pl.program_id(1)

    @pl.when(j == 0)
    def _():
        x = h_ref[...]
        xn_ref[...] = _rms(x, g_ref[...]).astype(BF16)
        o_ref[...] = x

    xn = xn_ref[...]
    gate = _dot(xn, wg_ref[...])
    up = _dot(xn, wu_ref[...])
    act = (gate * jax.nn.sigmoid(gate) * up * FFN_RES).astype(BF16)
    o_ref[...] += _dot(act, wd_ref[...])


def _ffn(h, g, wg, wu, wd):
    m = h.shape[0]
    return pl.pallas_call(
        _ffn_kernel,
        grid=(m // TM, D_FF // TF),
        in_specs=[
            pl.BlockSpec((TM, D_MODEL), lambda i, j: (i, 0)),
            pl.BlockSpec((1, D_MODEL), lambda i, j: (0, 0)),
            pl.BlockSpec((D_MODEL, TF), lambda i, j: (0, j)),
            pl.BlockSpec((D_MODEL, TF), lambda i, j: (0, j)),
            pl.BlockSpec((TF, D_MODEL), lambda i, j: (j, 0)),
        ],
        out_specs=pl.BlockSpec((TM, D_MODEL), lambda i, j: (i, 0)),
        out_shape=jax.ShapeDtypeStruct((m, D_MODEL), F32),
        scratch_shapes=[pltpu.VMEM((TM, D_MODEL), BF16)],
        compiler_params=_params("parallel", "arbitrary"),
        name="ffn",
    )(h, g, wg, wu, wd)


def _gla_proj_kernel(h_ref, g_ref, w_ref, wa1_ref, wa2_ref, ba_ref, proj_ref, la_ref, xn_ref):
    j = pl.program_id(1)

    @pl.when(j == 0)
    def _():
        xn = _rms(h_ref[...], g_ref[...]).astype(BF16)
        xn_ref[...] = xn
        a1 = _dot(xn, wa1_ref[...])
        z = _dot(a1.astype(BF16), wa2_ref[...]) + ba_ref[...]
        log_sig = jnp.minimum(z, 0.0) - jnp.log1p(jnp.exp(-jnp.abs(z)))
        la_ref[...] = log_sig * (1.0 / GLA_GATE_NORM)

    proj_ref[...] = _dot(xn_ref[...], w_ref[...]).astype(BF16)


def _gla_proj(h, g, w_main, w_a1, w_a2, b_a):
    m = h.shape[0]
    return pl.pallas_call(
        _gla_proj_kernel,
        grid=(m // TM_PROJ, GLA_MAIN // TN_PROJ),
        in_specs=[
            pl.BlockSpec((TM_PROJ, D_MODEL), lambda i, j: (i, 0)),
            pl.BlockSpec((1, D_MODEL), lambda i, j: (0, 0)),
            pl.BlockSpec((D_MODEL, TN_PROJ), lambda i, j: (0, j)),
            pl.BlockSpec((D_MODEL, V7X_LANES), lambda i, j: (0, 0)),
            pl.BlockSpec((V7X_LANES, GLA_QK), lambda i, j: (0, 0)),
            pl.BlockSpec((1, GLA_QK), lambda i, j: (0, 0)),
        ],
        out_specs=[
            pl.BlockSpec((TM_PROJ, TN_PROJ), lambda i, j: (i, j)),
            pl.BlockSpec((TM_PROJ, GLA_QK), lambda i, j: (i, 0)),
        ],
        out_shape=[
            jax.ShapeDtypeStruct((m, GLA_MAIN), BF16),
            jax.ShapeDtypeStruct((m, GLA_QK), F32),
        ],
        scratch_shapes=[pltpu.VMEM((TM_PROJ, D_MODEL), BF16)],
        compiler_params=_params("parallel", "arbitrary"),
        name="gla_proj",
    )(h, g, w_main, w_a1, w_a2, b_a)


def _gla_rec_kernel(q_ref, k_ref, v_ref, r_ref, la_ref, go_ref, o_ref, st_ref):
    c_sz = GLA_CHUNK

    @pl.when(pl.program_id(2) == 0)
    def _():
        st_ref[...] = jnp.zeros_like(st_ref)

    row = lax.broadcasted_iota(jnp.int32, (c_sz, c_sz), 0)
    col = lax.broadcasted_iota(jnp.int32, (c_sz, c_sz), 1)
    causal = col <= row
    tril = jnp.where(causal, 1.0, 0.0).astype(BF16)
    go = go_ref[...]

    def chunk(c, carry):
        sl = pl.ds(pl.multiple_of(c * c_sz, c_sz), c_sz)
        la = la_ref[sl, :]
        la1 = la.astype(BF16)
        rem = la - la1.astype(F32)
        la2 = rem.astype(BF16)
        la3 = (rem - la2.astype(F32)).astype(BF16)
        b = _dot(tril, la1) + _dot(tril, la2) + _dot(tril, la3)
        b_last = b[c_sz - 1:c_sz, :]
        q = q_ref[sl, :].astype(F32) * (GLA_DK ** -0.5)
        k = k_ref[sl, :].astype(F32)
        v = v_ref[sl, :]
        q_dec = (q * jnp.exp(b)).astype(BF16)
        k_inv = (k * jnp.exp(-b)).astype(BF16)
        k_dec = (k * jnp.exp(b_last - b)).astype(BF16)
        attn = jnp.where(causal, _dot_nt(q_dec, k_inv), 0.0)
        st = st_ref[...]
        o = _dot(attn.astype(BF16), v) + _dot_nt(q_dec, st.astype(BF16))
        st_ref[...] = st * jnp.exp(b_last) + _dot_tn(v, k_dec)
        r = r_ref[sl, :].astype(F32)
        o_ref[sl, :] = (_rms(o, go) * (r * jax.nn.sigmoid(r))).astype(BF16)
        return carry

    lax.fori_loop(0, T_GLA // c_sz, chunk, 0)


def _gla_rec(proj, log_a, g_o):
    bsz, seq, _ = proj.shape
    k_off = GLA_QK // GLA_DK
    v_off = 2 * GLA_QK // GLA_DV
    r_off = (2 * GLA_QK + GLA_V) // GLA_DV
    return pl.pallas_call(
        _gla_rec_kernel,
        grid=(bsz, GLA_HEADS, seq // T_GLA),
        in_specs=[
            pl.BlockSpec((None, T_GLA, GLA_DK), lambda b, h, t: (b, t, h)),
            pl.BlockSpec((None, T_GLA, GLA_DK), lambda b, h, t: (b, t, k_off + h)),
            pl.BlockSpec((None, T_GLA, GLA_DV), lambda b, h, t: (b, t, v_off + h)),
            pl.BlockSpec((None, T_GLA, GLA_DV), lambda b, h, t: (b, t, r_off + h)),
            pl.BlockSpec((None, T_GLA, GLA_DK), lambda b, h, t: (b, t, h)),
            pl.BlockSpec((1, GLA_DV), lambda b, h, t: (0, 0)),
        ],
        out_specs=pl.BlockSpec((None, T_GLA, GLA_DV), lambda b, h, t: (b, t, h)),
        out_shape=jax.ShapeDtypeStruct((bsz, seq, GLA_V), BF16),
        scratch_shapes=[pltpu.VMEM((GLA_DV, GLA_DK), F32)],
        compiler_params=_params("parallel", "parallel", "arbitrary"),
        name="gla_rec",
    )(proj, proj, proj, proj, log_a, g_o)


def _matmul_res_kernel(a_ref, w_ref, res_ref, o_ref):
    o_ref[...] = res_ref[...] + _dot(a_ref[...], w_ref[...])


def _matmul_res(a, w, res):
    m, kdim = a.shape
    n = w.shape[1]
    return pl.pallas_call(
        _matmul_res_kernel,
        grid=(m // TM,),
        in_specs=[
            pl.BlockSpec((TM, kdim), lambda i: (i, 0)),
            pl.BlockSpec((kdim, n), lambda i: (0, 0)),
            pl.BlockSpec((TM, n), lambda i: (i, 0)),
        ],
        out_specs=pl.BlockSpec((TM, n), lambda i: (i, 0)),
        out_shape=jax.ShapeDtypeStruct((m, n), F32),
        compiler_params=_params("parallel"),
        name="matmul_res",
    )(a, w, res)


def _head_pair_rms(t, g2, low):
    sq = t * t
    s_all = jnp.sum(sq, axis=-1, keepdims=True)
    s_lo = jnp.sum(jnp.where(low, sq, 0.0), axis=-1, keepdims=True)
    ms = jnp.where(low, s_lo, s_all - s_lo) * (1.0 / HEAD_DIM)
    return t * lax.rsqrt(ms + EPS) * g2


def _q_proj_kernel(h_ref, g_ref, w_ref, gq_ref, o_ref):
    q = _dot(_rms(h_ref[...], g_ref[...]).astype(BF16), w_ref[...])
    low = _low_half((q.shape[0], PAIR_W))
    gq = gq_ref[...]
    for pr in range(N_PAIRS):
        sl = slice(pr * PAIR_W, (pr + 1) * PAIR_W)
        o_ref[:, sl] = _head_pair_rms(q[:, sl], gq, low).astype(BF16)


def _q_proj(h, g, w, gq2):
    m = h.shape[0]
    return pl.pallas_call(
        _q_proj_kernel,
        grid=(m // TM,),
        in_specs=[
            pl.BlockSpec((TM, D_MODEL), lambda i: (i, 0)),
            pl.BlockSpec((1, D_MODEL), lambda i: (0, 0)),
            pl.BlockSpec((D_MODEL, D_MODEL), lambda i: (0, 0)),
            pl.BlockSpec((1, PAIR_W), lambda i: (0, 0)),
        ],
        out_specs=pl.BlockSpec((TM, D_MODEL), lambda i: (i, 0)),
        out_shape=jax.ShapeDtypeStruct((m, D_MODEL), BF16),
        compiler_params=_params("parallel"),
        name="q_proj",
    )(h, g, w, gq2)


def _kv_kernel(h_ref, g_ref, w_ref, gk_ref, klo_ref, khi_ref, ve_ref, vo_ref):
    kv = _dot(_rms(h_ref[...], g_ref[...]).astype(BF16), w_ref[...])
    gk = gk_ref[...]
    ones = jnp.ones((kv.shape[0], HEAD_DIM), F32)
    zeros = jnp.zeros((kv.shape[0], HEAD_DIM), F32)
    klo, khi, ve, vo = [], [], [], []
    for i in range(N_KV_HEADS):
        k = _rms(kv[:, i * HEAD_DIM:(i + 1) * HEAD_DIM], gk)
        v = kv[:, KV_W + i * HEAD_DIM:KV_W + (i + 1) * HEAD_DIM]
        klo += [k, zeros]
        khi += [zeros, k]
        ve += [v, ones]
        vo += [ones, v]
    klo_ref[...] = jnp.concatenate(klo, axis=1).astype(BF16)
    khi_ref[...] = jnp.concatenate(khi, axis=1).astype(BF16)
    ve_ref[...] = jnp.concatenate(ve, axis=1).astype(BF16)
    vo_ref[...] = jnp.concatenate(vo, axis=1).astype(BF16)


def _shared_kv(h, g, w, gk):
    m = h.shape[0]
    wide = N_KV_HEADS * PAIR_W
    out = jax.ShapeDtypeStruct((m, wide), BF16)
    return pl.pallas_call(
        _kv_kernel,
        grid=(m // TM,),
        in_specs=[
            pl.BlockSpec((TM, D_MODEL), lambda i: (i, 0)),
            pl.BlockSpec((1, D_MODEL), lambda i: (0, 0)),
            pl.BlockSpec((D_MODEL, 2 * KV_W), lambda i: (0, 0)),
            pl.BlockSpec((1, HEAD_DIM), lambda i: (0, 0)),
        ],
        out_specs=[pl.BlockSpec((TM, wide), lambda i: (i, 0))] * 4,
        out_shape=[out] * 4,
        compiler_params=_params("parallel"),
        name="shared_kv",
    )(h, g, w, gk)


def _slot_head(kv, slot):
    return kv * GROUP + 2 * (slot % GROUP_PAIRS) + slot // GROUP_PAIRS


def _bias_kernel(rb_ref, bucket_ref, o_ref):
    later = pl.program_id(0)
    h = _slot_head(pl.program_id(1), pl.program_id(2))
    bucket = bucket_ref[...]
    acc = jnp.zeros(bucket.shape, F32)
    for b in range(N_BUCKETS):
        acc = jnp.where(bucket == b, rb_ref[b, h], acc)
    qi = lax.broadcasted_iota(jnp.int32, bucket.shape, 0)
    sj = lax.broadcasted_iota(jnp.int32, bucket.shape, 1)
    dist = qi + BLOCK - sj
    kpos = later * BLOCK - BLOCK + sj
    mask = (dist >= 0) & (dist < WINDOW) & (kpos >= 0)
    o_ref[...] = jnp.where(mask, acc, -jnp.inf)


def _bias_table(rel_bias, bucket):
    return pl.pallas_call(
        _bias_kernel,
        grid=(2, N_KV_HEADS, GROUP),
        in_specs=[
            pl.BlockSpec(memory_space=pltpu.SMEM),
            pl.BlockSpec((BLOCK, 2 * BLOCK), lambda f, k, s: (0, 0)),
        ],
        out_specs=pl.BlockSpec((None, None, BLOCK, 2 * BLOCK), lambda f, k, s: (f, k, s, 0)),
        out_shape=jax.ShapeDtypeStruct((2, N_KV_HEADS, GROUP_ROWS, 2 * BLOCK), F32),
        compiler_params=_params("parallel", "parallel", "parallel"),
        name="bias_table",
    )(rel_bias, bucket)


def _swa_kernel(q_ref, klp_ref, klc_ref, khp_ref, khc_ref, vep_ref, vec_ref, vop_ref, voc_ref,
                bias_ref, sink_ref, o_ref):
    low = _low_half((BLOCK, PAIR_W))
    half = GROUP_ROWS // 2

    def softmax_numerator(s, sink):
        mx = jnp.maximum(jnp.max(s, axis=-1, keepdims=True), sink)
        return jnp.exp(s - mx).astype(BF16), jnp.exp(sink - mx)

    for kv in range(N_KV_HEADS):
        lanes = slice(kv * PAIR_W, (kv + 1) * PAIR_W)
        band = lambda prev, cur: jnp.concatenate([prev[:, lanes], cur[:, lanes]], axis=0)
        klo, khi = band(klp_ref, klc_ref), band(khp_ref, khc_ref)
        ve, vo = band(vep_ref, vec_ref), band(vop_ref, voc_ref)
        q0 = kv * GROUP_PAIRS * PAIR_W
        qs = jnp.concatenate([q_ref[:, q0 + pr * PAIR_W:q0 + (pr + 1) * PAIR_W]
                              for pr in range(GROUP_PAIRS)], axis=0)
        pe, ste = softmax_numerator(_dot_nt(qs, klo) + bias_ref[kv, :half], sink_ref[kv, :half])
        po, sto = softmax_numerator(_dot_nt(qs, khi) + bias_ref[kv, half:], sink_ref[kv, half:])
        oe = _dot(pe, ve)
        oo = _dot(po, vo)
        for pr in range(GROUP_PAIRS):
            rows = slice(pr * BLOCK, (pr + 1) * BLOCK)
            te, to = oe[rows], oo[rows]
            num = jnp.where(low, te, to)
            den = pltpu.roll(jnp.where(low, to, te), HEAD_DIM, 1)
            st = jnp.where(low, ste[rows], sto[rows])
            o_ref[:, q0 + pr * PAIR_W:q0 + (pr + 1) * PAIR_W] = (num / (den + st)).astype(BF16)


def _swa_attn(q, klo, khi, ve, vo, bias, sink_rows):
    bsz, seq, _ = q.shape
    nb = seq // BLOCK
    wide = N_KV_HEADS * PAIR_W
    prev = lambda b, n: (b, jnp.maximum(n - 1, 0), 0)
    cur = lambda b, n: (b, n, 0)
    kv_prev = pl.BlockSpec((None, BLOCK, wide), prev)
    kv_cur = pl.BlockSpec((None, BLOCK, wide), cur)
    return pl.pallas_call(
        _swa_kernel,
        grid=(bsz, nb),
        in_specs=[
            pl.BlockSpec((None, BLOCK, D_MODEL), cur),
            kv_prev, kv_cur, kv_prev, kv_cur, kv_prev, kv_cur, kv_prev, kv_cur,
            pl.BlockSpec((None, N_KV_HEADS, GROUP_ROWS, 2 * BLOCK),
                         lambda b, n: (jnp.minimum(n, 1), 0, 0, 0)),
            pl.BlockSpec((N_KV_HEADS, GROUP_ROWS, 1), lambda b, n: (0, 0, 0)),
        ],
        out_specs=pl.BlockSpec((None, BLOCK, D_MODEL), cur),
        out_shape=jax.ShapeDtypeStruct((bsz, seq, D_MODEL), BF16),
        compiler_params=_params("parallel", "arbitrary"),
        name="swa_attn",
    )(q, klo, klo, khi, khi, ve, ve, vo, vo, bias, sink_rows)


def _ple_kernel(h_ref, g_ref, wgate_ref, p_ref, wple_ref, o_ref):
    x = h_ref[...]
    gate = jax.nn.sigmoid(_dot(_rms(x, g_ref[...]).astype(BF16), wgate_ref[...]))
    o_ref[...] = x + gate * _dot(p_ref[...].astype(BF16), wple_ref[...])


def _ple(h, g, w_gate, p, w_ple):
    m = h.shape[0]
    return pl.pallas_call(
        _ple_kernel,
        grid=(m // TM,),
        in_specs=[
            pl.BlockSpec((TM, D_MODEL), lambda i: (i, 0)),
            pl.BlockSpec((1, D_MODEL), lambda i: (0, 0)),
            pl.BlockSpec((D_MODEL, D_MODEL), lambda i: (0, 0)),
            pl.BlockSpec((TM, PLE_DIM), lambda i: (i, 0)),
            pl.BlockSpec((PLE_DIM, D_MODEL), lambda i: (0, 0)),
        ],
        out_specs=pl.BlockSpec((TM, D_MODEL), lambda i: (i, 0)),
        out_shape=jax.ShapeDtypeStruct((m, D_MODEL), F32),
        compiler_params=_params("parallel"),
        name="ple",
    )(h, g, w_gate, p, w_ple)


def _t5_bucket(d):
    max_exact = N_BUCKETS // 2
    is_small = d < max_exact
    df = jnp.maximum(d, 1).astype(F32)
    large = max_exact + (jnp.log(df / max_exact) / math.log(MAX_DISTANCE / max_exact)
                         * (N_BUCKETS - max_exact)).astype(jnp.int32)
    large = jnp.minimum(large, N_BUCKETS - 1)
    return jnp.where(is_small, d, large)


def _sink_rows(sinks):
    per_slot = sinks.astype(F32).reshape(N_KV_HEADS, GROUP_PAIRS, 2).transpose(0, 2, 1)
    return jnp.repeat(per_slot.reshape(N_KV_HEADS, GROUP), BLOCK, axis=1)[..., None]


def kernel(x, p, norm_g, ffn_wg, ffn_wu, ffn_wd, ple_w, ple_gate_w, gla_w_in, gla_w_a2, gla_b_a,
           gla_norm_g, gla_w_o, kv_norm_g, w_kv, k_norm_g, rel_bias, swa_w_q, q_norm_g, sinks, swa_w_o):
    bsz, seq, d = x.shape
    depth = norm_g.shape[0]
    n_a = gla_w_in.shape[0]
    m = bsz * seq
    row = lambda v: v.reshape(1, -1)

    qi = jnp.arange(BLOCK)[:, None]
    sj = jnp.arange(2 * BLOCK)[None, :]
    bucket = _t5_bucket(jnp.maximum(qi + BLOCK - sj, 0)).astype(jnp.int32)
    bias = _bias_table(rel_bias, bucket)

    h = x.reshape(m, d)
    kv_tiles = None
    for i in range(depth):
        if i == n_a:
            kv_tiles = [t.reshape(bsz, seq, -1) for t in
                        _shared_kv(h, row(kv_norm_g), w_kv.astype(BF16), row(k_norm_g))]
        h = _ffn(h, row(norm_g[i, 0]), ffn_wg[i, 0].astype(BF16), ffn_wu[i, 0].astype(BF16),
                 ffn_wd[i, 0].astype(BF16))
        if i < n_a:
            w_in = gla_w_in[i]
            w_a1 = jnp.pad(w_in[:, GLA_MAIN:], ((0, 0), (0, V7X_LANES - GLA_RANK))).astype(BF16)
            w_a2 = jnp.pad(gla_w_a2[i], ((0, V7X_LANES - GLA_RANK), (0, 0))).astype(BF16)
            proj, log_a = _gla_proj(h, row(norm_g[i, 1]), w_in[:, :GLA_MAIN].astype(BF16), w_a1, w_a2,
                                    row(gla_b_a[i]))
            og = _gla_rec(proj.reshape(bsz, seq, GLA_MAIN), log_a.reshape(bsz, seq, GLA_QK),
                          row(gla_norm_g[i]))
            h = _matmul_res(og.reshape(m, GLA_V), gla_w_o[i].astype(BF16), h)
        else:
            j = i - n_a
            gq2 = row(jnp.tile(q_norm_g[j], 2)) * (HEAD_DIM ** -0.5)
            q = _q_proj(h, row(norm_g[i, 1]), swa_w_q[j].astype(BF16), gq2)
            o = _swa_attn(q.reshape(bsz, seq, d), *kv_tiles, bias, _sink_rows(sinks[j]))
            h = _matmul_res(o.reshape(m, d), swa_w_o[j].astype(BF16), h)
        h = _ffn(h, row(norm_g[i, 2]), ffn_wg[i, 1].astype(BF16), ffn_wu[i, 1].astype(BF16),
                 ffn_wd[i, 1].astype(BF16))
        h = _ple(h, row(norm_g[i, 3]), ple_gate_w[i].astype(BF16), p[i].reshape(m, PLE_DIM),
                 ple_w[i].astype(BF16))
    return h.reshape(bsz, seq, d)
```

```python
import math

import jax
import jax.numpy as jnp
from jax import lax
from jax.experimental import pallas as pl
from jax.experimental.pallas import tpu as pltpu

F32 = jnp.float32
BF16 = jnp.bfloat16

D_MODEL = 2048
D_FF = 5632
FFN_RES = 0.5
PLE_DIM = 256
EPS = 1e-6
GLA_HEADS = 4
GLA_QK = 1024
GLA_V = 2048
GLA_DK = 256
GLA_DV = 512
GLA_RANK = 16
GLA_GATE_NORM = 16.0
GLA_CHUNK = 64
GLA_MAIN = 2 * GLA_QK + 2 * GLA_V
HEAD_DIM = 64
N_Q_HEADS = 32
N_KV_HEADS = 4
GROUP = 8
KV_W = N_KV_HEADS * HEAD_DIM
WINDOW = 128
BLOCK = WINDOW
N_BUCKETS = 32
MAX_DISTANCE = WINDOW

V7X_LANES = 128
V7X_VMEM_BYTES = 64 * 1024 * 1024
VMEM_LIMIT = V7X_VMEM_BYTES - 8 * 1024 * 1024

PAIR_W = 2 * HEAD_DIM
assert PAIR_W == V7X_LANES
N_PAIRS = N_Q_HEADS // 2
GROUP_PAIRS = GROUP // 2
GROUP_ROWS = GROUP * BLOCK

TM = 512
TM_FFN = 1024
TF = 512
TM_PROJ = 1024
TN_PROJ = 1024
T_GLA = 512


def _params(*sem):
    return pltpu.CompilerParams(dimension_semantics=sem, vmem_limit_bytes=VMEM_LIMIT)


def _rms(x, g):
    return x * lax.rsqrt(jnp.mean(x * x, axis=-1, keepdims=True) + EPS) * g


def _dot(a, b):
    return jnp.dot(a, b, preferred_element_type=F32)


def _dot_nt(a, b):
    return lax.dot_general(a, b, (((1,), (1,)), ((), ())), preferred_element_type=F32)


def _dot_tn(a, b):
    return lax.dot_general(a, b, (((0,), (0,)), ((), ())), preferred_element_type=F32)


def _low_half(shape):
    return lax.broadcasted_iota(jnp.int32, shape, len(shape) - 1) < HEAD_DIM


def _ffn_kernel(h_hbm, g_ref, wg_ref, wu_ref, wd_ref, o_ref, x_ref, xn_ref, x_sem):
    i = pl.program_id(0)
    j = pl.program_id(1)

    def x_copy(tile):
        rows = pl.ds(pl.multiple_of(tile * TM_FFN, TM_FFN), TM_FFN)
        return pltpu.make_async_copy(h_hbm.at[rows], x_ref, x_sem)

    @pl.when(j == 0)
    def _():
        @pl.when(i == 0)
        def _():
            x_copy(0).start()

        x_copy(i).wait()
        x = x_ref[...]
        xn_ref[...] = _rms(x, g_ref[...]).astype(BF16)
        o_ref[...] = x

    @pl.when((j == 1) & (i + 1 < pl.num_programs(0)))
    def _():
        x_copy(i + 1).start()

    xn = xn_ref[...]
    gate = _dot(xn, wg_ref[...])
    up = _dot(xn, wu_ref[...])
    act = (gate * jax.nn.sigmoid(gate) * up * FFN_RES).astype(BF16)
    o_ref[...] += _dot(act, wd_ref[...])


def _ffn(h, g, wg, wu, wd):
    m = h.shape[0]
    return pl.pallas_call(
        _ffn_kernel,
        grid=(m // TM_FFN, D_FF // TF),
        in_specs=[
            pl.BlockSpec(memory_space=pl.ANY),
            pl.BlockSpec((1, D_MODEL), lambda i, j: (0, 0)),
            pl.BlockSpec((D_MODEL, TF), lambda i, j: (0, j)),
            pl.BlockSpec((D_MODEL, TF), lambda i, j: (0, j)),
            pl.BlockSpec((TF, D_MODEL), lambda i, j: (j, 0)),
        ],
        out_specs=pl.BlockSpec((TM_FFN, D_MODEL), lambda i, j: (i, 0)),
        out_shape=jax.ShapeDtypeStruct((m, D_MODEL), F32),
        scratch_shapes=[
            pltpu.VMEM((TM_FFN, D_MODEL), F32),
            pltpu.VMEM((TM_FFN, D_MODEL), BF16),
            pltpu.SemaphoreType.DMA,
        ],
        compiler_params=_params("arbitrary", "arbitrary"),
        name="ffn",
    )(h, g, wg, wu, wd)


def _gla_proj_kernel(h_ref, g_ref, w_ref, wa1_ref, wa2_ref, ba_ref, proj_ref, la_ref, xn_ref):
    j = pl.program_id(1)

    @pl.when(j == 0)
    def _():
        xn = _rms(h_ref[...], g_ref[...]).astype(BF16)
        xn_ref[...] = xn
        a1 = _dot(xn, wa1_ref[...])
        z = _dot(a1.astype(BF16), wa2_ref[...]) + ba_ref[...]
        log_sig = jnp.minimum(z, 0.0) - jnp.log1p(jnp.exp(-jnp.abs(z)))
        la_ref[...] = log_sig * (1.0 / GLA_GATE_NORM)

    proj_ref[...] = _dot(xn_ref[...], w_ref[...]).astype(BF16)


def _gla_proj(h, g, w_main, w_a1, w_a2, b_a):
    m = h.shape[0]
    return pl.pallas_call(
        _gla_proj_kernel,
        grid=(m // TM_PROJ, GLA_MAIN // TN_PROJ),
        in_specs=[
            pl.BlockSpec((TM_PROJ, D_MODEL), lambda i, j: (i, 0)),
            pl.BlockSpec((1, D_MODEL), lambda i, j: (0, 0)),
            pl.BlockSpec((D_MODEL, TN_PROJ), lambda i, j: (0, j)),
            pl.BlockSpec((D_MODEL, V7X_LANES), lambda i, j: (0, 0)),
            pl.BlockSpec((V7X_LANES, GLA_QK), lambda i, j: (0, 0)),
            pl.BlockSpec((1, GLA_QK), lambda i, j: (0, 0)),
        ],
        out_specs=[
            pl.BlockSpec((TM_PROJ, TN_PROJ), lambda i, j: (i, j)),
            pl.BlockSpec((TM_PROJ, GLA_QK), lambda i, j: (i, 0)),
        ],
        out_shape=[
            jax.ShapeDtypeStruct((m, GLA_MAIN), BF16),
            jax.ShapeDtypeStruct((m, GLA_QK), F32),
        ],
        scratch_shapes=[pltpu.VMEM((TM_PROJ, D_MODEL), BF16)],
        compiler_params=_params("parallel", "arbitrary"),
        name="gla_proj",
    )(h, g, w_main, w_a1, w_a2, b_a)


def _gla_rec_kernel(q_ref, k_ref, v_ref, r_ref, la_ref, go_ref, o_ref, st_ref):
    c_sz = GLA_CHUNK

    @pl.when(pl.program_id(1) == 0)
    def _():
        st_ref[...] = jnp.zeros_like(st_ref)

    row = lax.broadcasted_iota(jnp.int32, (c_sz, c_sz), 0)
    col = lax.broadcasted_iota(jnp.int32, (c_sz, c_sz), 1)
    causal = col <= row
    tril = jnp.where(causal, 1.0, 0.0).astype(BF16)
    go = go_ref[...]

    def chunk(c, carry):
        sl = pl.ds(pl.multiple_of(c * c_sz, c_sz), c_sz)
        la = la_ref[sl, :]
        la1 = la.astype(BF16)
        rem = la - la1.astype(F32)
        la2 = rem.astype(BF16)
        la3 = (rem - la2.astype(F32)).astype(BF16)
        b_all = _dot(tril, la1) + _dot(tril, la2) + _dot(tril, la3)
        for h in range(GLA_HEADS):
            kc = slice(h * GLA_DK, (h + 1) * GLA_DK)
            vc = slice(h * GLA_DV, (h + 1) * GLA_DV)
            b = b_all[:, kc]
            b_last = b[c_sz - 1:c_sz, :]
            q = q_ref[sl, kc].astype(F32) * (GLA_DK ** -0.5)
            k = k_ref[sl, kc].astype(F32)
            v = v_ref[sl, vc]
            q_dec = (q * jnp.exp(b)).astype(BF16)
            k_inv = (k * jnp.exp(-b)).astype(BF16)
            k_dec = (k * jnp.exp(b_last - b)).astype(BF16)
            attn = jnp.where(causal, _dot_nt(q_dec, k_inv), 0.0)
            st = st_ref[h]
            o = _dot(attn.astype(BF16), v) + _dot_nt(q_dec, st.astype(BF16))
            st_ref[h] = st * jnp.exp(b_last) + _dot_tn(v, k_dec)
            r = r_ref[sl, vc].astype(F32)
            o_ref[sl, vc] = (_rms(o, go) * (r * jax.nn.sigmoid(r))).astype(BF16)
        return carry

    lax.fori_loop(0, T_GLA // c_sz, chunk, 0)


def _gla_rec(proj, log_a, g_o):
    bsz, seq, _ = proj.shape
    v_blk = 2 * GLA_QK // GLA_V
    return pl.pallas_call(
        _gla_rec_kernel,
        grid=(bsz, seq // T_GLA),
        in_specs=[
            pl.BlockSpec((None, T_GLA, GLA_QK), lambda b, t: (b, t, 0)),
            pl.BlockSpec((None, T_GLA, GLA_QK), lambda b, t: (b, t, 1)),
            pl.BlockSpec((None, T_GLA, GLA_V), lambda b, t: (b, t, v_blk)),
            pl.BlockSpec((None, T_GLA, GLA_V), lambda b, t: (b, t, v_blk + 1)),
            pl.BlockSpec((None, T_GLA, GLA_QK), lambda b, t: (b, t, 0)),
            pl.BlockSpec((1, GLA_DV), lambda b, t: (0, 0)),
        ],
        out_specs=pl.BlockSpec((None, T_GLA, GLA_V), lambda b, t: (b, t, 0)),
        out_shape=jax.ShapeDtypeStruct((bsz, seq, GLA_V), BF16),
        scratch_shapes=[pltpu.VMEM((GLA_HEADS, GLA_DV, GLA_DK), F32)],
        compiler_params=_params("parallel", "arbitrary"),
        name="gla_rec",
    )(proj, proj, proj, proj, log_a, g_o)


def _matmul_res_kernel(a_ref, w_ref, res_ref, o_ref):
    o_ref[...] = res_ref[...] + _dot(a_ref[...], w_ref[...])


def _matmul_res(a, w, res):
    m, kdim = a.shape
    n = w.shape[1]
    return pl.pallas_call(
        _matmul_res_kernel,
        grid=(m // TM,),
        in_specs=[
            pl.BlockSpec((TM, kdim), lambda i: (i, 0)),
            pl.BlockSpec((kdim, n), lambda i: (0, 0)),
            pl.BlockSpec((TM, n), lambda i: (i, 0)),
        ],
        out_specs=pl.BlockSpec((TM, n), lambda i: (i, 0)),
        out_shape=jax.ShapeDtypeStruct((m, n), F32),
        compiler_params=_params("parallel"),
        name="matmul_res",
    )(a, w, res)


def _head_pair_rms(t, g2, low):
    sq = t * t
    s_all = jnp.sum(sq, axis=-1, keepdims=True)
    s_lo = jnp.sum(jnp.where(low, sq, 0.0), axis=-1, keepdims=True)
    ms = jnp.where(low, s_lo, s_all - s_lo) * (1.0 / HEAD_DIM)
    return t * lax.rsqrt(ms + EPS) * g2


def _q_proj_kernel(h_ref, g_ref, w_ref, gq_ref, o_ref):
    q = _dot(_rms(h_ref[...], g_ref[...]).astype(BF16), w_ref[...])
    low = _low_half((q.shape[0], PAIR_W))
    gq = gq_ref[...]
    for pr in range(N_PAIRS):
        sl = slice(pr * PAIR_W, (pr + 1) * PAIR_W)
        o_ref[:, sl] = _head_pair_rms(q[:, sl], gq, low).astype(BF16)


def _q_proj(h, g, w, gq2):
    m = h.shape[0]
    return pl.pallas_call(
        _q_proj_kernel,
        grid=(m // TM,),
        in_specs=[
            pl.BlockSpec((TM, D_MODEL), lambda i: (i, 0)),
            pl.BlockSpec((1, D_MODEL), lambda i: (0, 0)),
            pl.BlockSpec((D_MODEL, D_MODEL), lambda i: (0, 0)),
            pl.BlockSpec((1, PAIR_W), lambda i: (0, 0)),
        ],
        out_specs=pl.BlockSpec((TM, D_MODEL), lambda i: (i, 0)),
        out_shape=jax.ShapeDtypeStruct((m, D_MODEL), BF16),
        compiler_params=_params("parallel"),
        name="q_proj",
    )(h, g, w, gq2)


def _kv_kernel(h_ref, g_ref, w_ref, gk_ref, klo_ref, khi_ref, ve_ref, vo_ref):
    kv = _dot(_rms(h_ref[...], g_ref[...]).astype(BF16), w_ref[...])
    gk = gk_ref[...]
    ones = jnp.ones((kv.shape[0], HEAD_DIM), F32)
    zeros = jnp.zeros((kv.shape[0], HEAD_DIM), F32)
    klo, khi, ve, vo = [], [], [], []
    for i in range(N_KV_HEADS):
        k = _rms(kv[:, i * HEAD_DIM:(i + 1) * HEAD_DIM], gk)
        v = kv[:, KV_W + i * HEAD_DIM:KV_W + (i + 1) * HEAD_DIM]
        klo += [k, zeros]
        khi += [zeros, k]
        ve += [v, ones]
        vo += [ones, v]
    klo_ref[...] = jnp.concatenate(klo, axis=1).astype(BF16)
    khi_ref[...] = jnp.concatenate(khi, axis=1).astype(BF16)
    ve_ref[...] = jnp.concatenate(ve, axis=1).astype(BF16)
    vo_ref[...] = jnp.concatenate(vo, axis=1).astype(BF16)


def _shared_kv(h, g, w, gk):
    m = h.shape[0]
    wide = N_KV_HEADS * PAIR_W
    out = jax.ShapeDtypeStruct((m, wide), BF16)
    return pl.pallas_call(
        _kv_kernel,
        grid=(m // TM,),
        in_specs=[
            pl.BlockSpec((TM, D_MODEL), lambda i: (i, 0)),
            pl.BlockSpec((1, D_MODEL), lambda i: (0, 0)),
            pl.BlockSpec((D_MODEL, 2 * KV_W), lambda i: (0, 0)),
            pl.BlockSpec((1, HEAD_DIM), lambda i: (0, 0)),
        ],
        out_specs=[pl.BlockSpec((TM, wide), lambda i: (i, 0))] * 4,
        out_shape=[out] * 4,
        compiler_params=_params("parallel"),
        name="shared_kv",
    )(h, g, w, gk)


def _slot_head(kv, slot):
    return kv * GROUP + 2 * (slot % GROUP_PAIRS) + slot // GROUP_PAIRS


def _bias_kernel(rb_ref, bucket_ref, o_ref):
    later = pl.program_id(0)
    h = _slot_head(pl.program_id(1), pl.program_id(2))
    bucket = bucket_ref[...]
    acc = jnp.zeros(bucket.shape, F32)
    for b in range(N_BUCKETS):
        acc = jnp.where(bucket == b, rb_ref[b, h], acc)
    qi = lax.broadcasted_iota(jnp.int32, bucket.shape, 0)
    sj = lax.broadcasted_iota(jnp.int32, bucket.shape, 1)
    dist = qi + BLOCK - sj
    kpos = later * BLOCK - BLOCK + sj
    mask = (dist >= 0) & (dist < WINDOW) & (kpos >= 0)
    o_ref[...] = jnp.where(mask, acc, -jnp.inf)


def _bias_table(rel_bias, bucket):
    return pl.pallas_call(
        _bias_kernel,
        grid=(2, N_KV_HEADS, GROUP),
        in_specs=[
            pl.BlockSpec(memory_space=pltpu.SMEM),
            pl.BlockSpec((BLOCK, 2 * BLOCK), lambda f, k, s: (0, 0)),
        ],
        out_specs=pl.BlockSpec((None, None, BLOCK, 2 * BLOCK), lambda f, k, s: (f, k, s, 0)),
        out_shape=jax.ShapeDtypeStruct((2, N_KV_HEADS, GROUP_ROWS, 2 * BLOCK), F32),
        compiler_params=_params("parallel", "parallel", "parallel"),
        name="bias_table",
    )(rel_bias, bucket)


def _swa_kernel(q_ref, klp_ref, klc_ref, khp_ref, khc_ref, vep_ref, vec_ref, vop_ref, voc_ref,
                bias_ref, sink_ref, o_ref):
    low = _low_half((BLOCK, PAIR_W))
    half = GROUP_ROWS // 2

    def softmax_numerator(s, sink):
        mx = jnp.maximum(jnp.max(s, axis=-1, keepdims=True), sink)
        return jnp.exp(s - mx).astype(BF16), jnp.exp(sink - mx)

    for kv in range(N_KV_HEADS):
        lanes = slice(kv * PAIR_W, (kv + 1) * PAIR_W)
        band = lambda prev, cur: jnp.concatenate([prev[:, lanes], cur[:, lanes]], axis=0)
        klo, khi = band(klp_ref, klc_ref), band(khp_ref, khc_ref)
        ve, vo = band(vep_ref, vec_ref), band(vop_ref, voc_ref)
        q0 = kv * GROUP_PAIRS * PAIR_W
        qs = jnp.concatenate([q_ref[:, q0 + pr * PAIR_W:q0 + (pr + 1) * PAIR_W]
                              for pr in range(GROUP_PAIRS)], axis=0)
        pe, ste = softmax_numerator(_dot_nt(qs, klo) + bias_ref[kv, :half], sink_ref[kv, :half])
        po, sto = softmax_numerator(_dot_nt(qs, khi) + bias_ref[kv, half:], sink_ref[kv, half:])
        oe = _dot(pe, ve)
        oo = _dot(po, vo)
        for pr in range(GROUP_PAIRS):
            rows = slice(pr * BLOCK, (pr + 1) * BLOCK)
            te, to = oe[rows], oo[rows]
            num = jnp.where(low, te, to)
            den = pltpu.roll(jnp.where(low, to, te), HEAD_DIM, 1)
            st = jnp.where(low, ste[rows], sto[rows])
            o_ref[:, q0 + pr * PAIR_W:q0 + (pr + 1) * PAIR_W] = (num / (den + st)).astype(BF16)


def _swa_attn(q, klo, khi, ve, vo, bias, sink_rows):
    bsz, seq, _ = q.shape
    nb = seq // BLOCK
    wide = N_KV_HEADS * PAIR_W
    prev = lambda b, n: (b, jnp.maximum(n - 1, 0), 0)
    cur = lambda b, n: (b, n, 0)
    kv_prev = pl.BlockSpec((None, BLOCK, wide), prev)
    kv_cur = pl.BlockSpec((None, BLOCK, wide), cur)
    return pl.pallas_call(
        _swa_kernel,
        grid=(bsz, nb),
        in_specs=[
            pl.BlockSpec((None, BLOCK, D_MODEL), cur),
            kv_prev, kv_cur, kv_prev, kv_cur, kv_prev, kv_cur, kv_prev, kv_cur,
            pl.BlockSpec((None, N_KV_HEADS, GROUP_ROWS, 2 * BLOCK),
                         lambda b, n: (jnp.minimum(n, 1), 0, 0, 0)),
            pl.BlockSpec((N_KV_HEADS, GROUP_ROWS, 1), lambda b, n: (0, 0, 0)),
        ],
        out_specs=pl.BlockSpec((None, BLOCK, D_MODEL), cur),
        out_shape=jax.ShapeDtypeStruct((bsz, seq, D_MODEL), BF16),
        compiler_params=_params("parallel", "arbitrary"),
        name="swa_attn",
    )(q, klo, klo, khi, khi, ve, ve, vo, vo, bias, sink_rows)


def _ple_kernel(h_ref, g_ref, wgate_ref, p_ref, wple_ref, o_ref):
    x = h_ref[...]
    gate = jax.nn.sigmoid(_dot(_rms(x, g_ref[...]).astype(BF16), wgate_ref[...]))
    o_ref[...] = x + gate * _dot(p_ref[...].astype(BF16), wple_ref[...])


def _ple(h, g, w_gate, p, w_ple):
    m = h.shape[0]
    return pl.pallas_call(
        _ple_kernel,
        grid=(m // TM,),
        in_specs=[
            pl.BlockSpec((TM, D_MODEL), lambda i: (i, 0)),
            pl.BlockSpec((1, D_MODEL), lambda i: (0, 0)),
            pl.BlockSpec((D_MODEL, D_MODEL), lambda i: (0, 0)),
            pl.BlockSpec((TM, PLE_DIM), lambda i: (i, 0)),
            pl.BlockSpec((PLE_DIM, D_MODEL), lambda i: (0, 0)),
        ],
        out_specs=pl.BlockSpec((TM, D_MODEL), lambda i: (i, 0)),
        out_shape=jax.ShapeDtypeStruct((m, D_MODEL), F32),
        compiler_params=_params("parallel"),
        name="ple",
    )(h, g, w_gate, p, w_ple)


def _t5_bucket(d):
    max_exact = N_BUCKETS // 2
    is_small = d < max_exact
    df = jnp.maximum(d, 1).astype(F32)
    large = max_exact + (jnp.log(df / max_exact) / math.log(MAX_DISTANCE / max_exact)
                         * (N_BUCKETS - max_exact)).astype(jnp.int32)
    large = jnp.minimum(large, N_BUCKETS - 1)
    return jnp.where(is_small, d, large)


def _sink_rows(sinks):
    per_slot = sinks.astype(F32).reshape(N_KV_HEADS, GROUP_PAIRS, 2).transpose(0, 2, 1)
    return jnp.repeat(per_slot.reshape(N_KV_HEADS, GROUP), BLOCK, axis=1)[..., None]


def kernel(x, p, norm_g, ffn_wg, ffn_wu, ffn_wd, ple_w, ple_gate_w, gla_w_in, gla_w_a2, gla_b_a,
           gla_norm_g, gla_w_o, kv_norm_g, w_kv, k_norm_g, rel_bias, swa_w_q, q_norm_g, sinks, swa_w_o):
    bsz, seq, d = x.shape
    depth = norm_g.shape[0]
    n_a = gla_w_in.shape[0]
    m = bsz * seq
    row = lambda v: v.reshape(1, -1)

    qi = jnp.arange(BLOCK)[:, None]
    sj = jnp.arange(2 * BLOCK)[None, :]
    bucket = _t5_bucket(jnp.maximum(qi + BLOCK - sj, 0)).astype(jnp.int32)
    bias = _bias_table(rel_bias, bucket)

    h = x.reshape(m, d)
    kv_tiles = None
    for i in range(depth):
        if i == n_a:
            kv_tiles = [t.reshape(bsz, seq, -1) for t in
                        _shared_kv(h, row(kv_norm_g), w_kv.astype(BF16), row(k_norm_g))]
        h = _ffn(h, row(norm_g[i, 0]), ffn_wg[i, 0].astype(BF16), ffn_wu[i, 0].astype(BF16),
                 ffn_wd[i, 0].astype(BF16))
        if i < n_a:
            w_in = gla_w_in[i]
            w_a1 = jnp.pad(w_in[:, GLA_MAIN:], ((0, 0), (0, V7X_LANES - GLA_RANK))).astype(BF16)
            w_a2 = jnp.pad(gla_w_a2[i], ((0, V7X_LANES - GLA_RANK), (0, 0))).astype(BF16)
            proj, log_a = _gla_proj(h, row(norm_g[i, 1]), w_in[:, :GLA_MAIN].astype(BF16), w_a1, w_a2,
                                    row(gla_b_a[i]))
            og = _gla_rec(proj.reshape(bsz, seq, GLA_MAIN), log_a.reshape(bsz, seq, GLA_QK),
                          row(gla_norm_g[i]))
            h = _matmul_res(og.reshape(m, GLA_V), gla_w_o[i].astype(BF16), h)
        else:
            j = i - n_a
            gq2 = row(jnp.tile(q_norm_g[j], 2)) * (HEAD_DIM ** -0.5)
            q = _q_proj(h, row(norm_g[i, 1]), swa_w_q[j].astype(BF16), gq2)
            o = _swa_attn(q.reshape(bsz, seq, d), *kv_tiles, bias, _sink_rows(sinks[j]))
            h = _matmul_res(o.reshape(m, d), swa_w_o[j].astype(BF16), h)
        h = _ffn(h, row(norm_g[i, 2]), ffn_wg[i, 1].astype(BF16), ffn_wu[i, 1].astype(BF16),
                 ffn_wd[i, 1].astype(BF16))
        h = _ple(h, row(norm_g[i, 3]), ple_gate_w[i].astype(BF16), p[i].reshape(m, PLE_DIM),
                 ple_w[i].astype(BF16))
    return h.reshape(bsz, seq, d)
```

```python
import functools
import math

import jax
import jax.numpy as jnp
from jax import lax
from jax.experimental import pallas as pl
from jax.experimental.pallas import tpu as pltpu

F32 = jnp.float32
BF16 = jnp.bfloat16

D_MODEL = 2048
D_FF = 5632
FFN_RES = 0.5
PLE_DIM = 256
EPS = 1e-6
GLA_HEADS = 4
GLA_QK = 1024
GLA_V = 2048
GLA_DK = 256
GLA_DV = 512
GLA_RANK = 16
GLA_GATE_NORM = 16.0
GLA_CHUNK = 64
GLA_MAIN = 2 * GLA_QK + 2 * GLA_V
HEAD_DIM = 64
N_Q_HEADS = 32
N_KV_HEADS = 4
GROUP = 8
KV_W = N_KV_HEADS * HEAD_DIM
WINDOW = 128
BLOCK = WINDOW
N_BUCKETS = 32
MAX_DISTANCE = WINDOW

V7X_LANES = 128
V7X_VMEM_BYTES = 64 * 1024 * 1024
VMEM_LIMIT = V7X_VMEM_BYTES - 8 * 1024 * 1024

PAIR_W = 2 * HEAD_DIM
assert PAIR_W == V7X_LANES
N_PAIRS = N_Q_HEADS // 2
GROUP_PAIRS = GROUP // 2
GROUP_ROWS = GROUP * BLOCK

TM = 512
TM_FFN = 1024
TF = 512
TM_PROJ = 1024
TN_PROJ = 1536
GATE_COLS = GLA_QK // (GLA_MAIN // TN_PROJ)
T_GLA = 512


def _params(*sem):
    return pltpu.CompilerParams(dimension_semantics=sem, vmem_limit_bytes=VMEM_LIMIT)


def _rms(x, g):
    return x * lax.rsqrt(jnp.mean(x * x, axis=-1, keepdims=True) + EPS) * g


def _dot(a, b):
    return jnp.dot(a, b, preferred_element_type=F32)


def _dot_nt(a, b):
    return lax.dot_general(a, b, (((1,), (1,)), ((), ())), preferred_element_type=F32)


def _dot_tn(a, b):
    return lax.dot_general(a, b, (((0,), (0,)), ((), ())), preferred_element_type=F32)


def _low_half(shape):
    return lax.broadcasted_iota(jnp.int32, shape, len(shape) - 1) < HEAD_DIM


def _ffn_kernel(h_hbm, g_ref, wg_ref, wu_ref, wd_ref, *rest, cast_next):
    if cast_next:
        nwg_ref, nwu_ref, nwd_ref, o_ref, nwg_out, nwu_out, nwd_out, x_ref, xn_ref, x_sem = rest
        nwg_out[...] = nwg_ref[...].astype(BF16)
        nwu_out[...] = nwu_ref[...].astype(BF16)
        nwd_out[...] = nwd_ref[...].astype(BF16)
    else:
        o_ref, x_ref, xn_ref, x_sem = rest
    i = pl.program_id(0)
    j = pl.program_id(1)

    def x_copy(tile):
        rows = pl.ds(pl.multiple_of(tile * TM_FFN, TM_FFN), TM_FFN)
        return pltpu.make_async_copy(h_hbm.at[rows], x_ref, x_sem)

    @pl.when(j == 0)
    def _():
        @pl.when(i == 0)
        def _():
            x_copy(0).start()

        x_copy(i).wait()
        x = x_ref[...]
        xn_ref[...] = _rms(x, g_ref[...]).astype(BF16)
        o_ref[...] = x

    @pl.when((j == 1) & (i + 1 < pl.num_programs(0)))
    def _():
        x_copy(i + 1).start()

    xn = xn_ref[...]
    gate = _dot(xn, wg_ref[...])
    up = _dot(xn, wu_ref[...])
    act = (gate * jax.nn.sigmoid(gate) * up * FFN_RES).astype(BF16)
    o_ref[...] += _dot(act, wd_ref[...])


def _ffn(h, g, wg, wu, wd, next_weights=None):
    m = h.shape[0]
    n_i, n_j = m // TM_FFN, D_FF // TF
    in_specs = [
        pl.BlockSpec(memory_space=pl.ANY),
        pl.BlockSpec((1, D_MODEL), lambda i, j: (0, 0)),
        pl.BlockSpec((D_MODEL, TF), lambda i, j: (0, j)),
        pl.BlockSpec((D_MODEL, TF), lambda i, j: (0, j)),
        pl.BlockSpec((TF, D_MODEL), lambda i, j: (j, 0)),
    ]
    out_specs = [pl.BlockSpec((TM_FFN, D_MODEL), lambda i, j: (i, 0))]
    out_shape = [jax.ShapeDtypeStruct((m, D_MODEL), F32)]
    args = [h, g, wg, wu, wd]
    if next_weights is not None:
        nwg, nwu, nwd, layer, sub = next_weights
        rows = D_MODEL // n_i
        in_specs += [
            pl.BlockSpec((None, None, rows, TF), lambda i, j: (layer, sub, i, j)),
            pl.BlockSpec((None, None, rows, TF), lambda i, j: (layer, sub, i, j)),
            pl.BlockSpec((None, None, TF, rows), lambda i, j: (layer, sub, j, i)),
        ]
        out_specs += [
            pl.BlockSpec((rows, TF), lambda i, j: (i, j)),
            pl.BlockSpec((rows, TF), lambda i, j: (i, j)),
            pl.BlockSpec((TF, rows), lambda i, j: (j, i)),
        ]
        out_shape += [
            jax.ShapeDtypeStruct((D_MODEL, D_FF), BF16),
            jax.ShapeDtypeStruct((D_MODEL, D_FF), BF16),
            jax.ShapeDtypeStruct((D_FF, D_MODEL), BF16),
        ]
        args += [nwg, nwu, nwd]
    outs = pl.pallas_call(
        functools.partial(_ffn_kernel, cast_next=next_weights is not None),
        grid=(n_i, n_j),
        in_specs=in_specs,
        out_specs=out_specs,
        out_shape=out_shape,
        scratch_shapes=[
            pltpu.VMEM((TM_FFN, D_MODEL), F32),
            pltpu.VMEM((TM_FFN, D_MODEL), BF16),
            pltpu.SemaphoreType.DMA,
        ],
        compiler_params=_params("arbitrary", "arbitrary"),
        name="ffn",
    )(*args)
    return outs[0], tuple(outs[1:])


def _gla_proj_kernel(h_ref, g_ref, w_ref, wa1_ref, wa2_ref, ba_ref, proj_ref, la_ref, xn_ref, a1_ref):
    @pl.when(pl.program_id(1) == 0)
    def _():
        xn = _rms(h_ref[...], g_ref[...]).astype(BF16)
        xn_ref[...] = xn
        a1_ref[...] = _dot(xn, wa1_ref[...]).astype(BF16)

    z = _dot(a1_ref[...], wa2_ref[...]) + ba_ref[...]
    log_sig = jnp.minimum(z, 0.0) - jnp.log1p(jnp.exp(-jnp.abs(z)))
    la_ref[...] = log_sig * (1.0 / GLA_GATE_NORM)
    proj_ref[...] = _dot(xn_ref[...], w_ref[...]).astype(BF16)


def _gla_proj(h, g, w_in_all, layer, w_a1, w_a2, b_a):
    m = h.shape[0]
    return pl.pallas_call(
        _gla_proj_kernel,
        grid=(m // TM_PROJ, GLA_MAIN // TN_PROJ),
        in_specs=[
            pl.BlockSpec((TM_PROJ, D_MODEL), lambda i, j: (i, 0)),
            pl.BlockSpec((1, D_MODEL), lambda i, j: (0, 0)),
            pl.BlockSpec((None, D_MODEL, TN_PROJ), lambda i, j: (layer, 0, j)),
            pl.BlockSpec((D_MODEL, V7X_LANES), lambda i, j: (0, 0)),
            pl.BlockSpec((V7X_LANES, GATE_COLS), lambda i, j: (0, j)),
            pl.BlockSpec((1, GATE_COLS), lambda i, j: (0, j)),
        ],
        out_specs=[
            pl.BlockSpec((TM_PROJ, TN_PROJ), lambda i, j: (i, j)),
            pl.BlockSpec((TM_PROJ, GATE_COLS), lambda i, j: (i, j)),
        ],
        out_shape=[
            jax.ShapeDtypeStruct((m, GLA_MAIN), BF16),
            jax.ShapeDtypeStruct((m, GLA_QK), F32),
        ],
        scratch_shapes=[pltpu.VMEM((TM_PROJ, D_MODEL), BF16), pltpu.VMEM((TM_PROJ, V7X_LANES), BF16)],
        compiler_params=_params("parallel", "arbitrary"),
        name="gla_proj",
    )(h, g, w_in_all, w_a1, w_a2, b_a)


def _gla_rec_kernel(q_ref, k_ref, v_ref, r_ref, la_ref, go_ref, o_ref, st_ref):
    c_sz = GLA_CHUNK

    @pl.when(pl.program_id(1) == 0)
    def _():
        st_ref[...] = jnp.zeros_like(st_ref)

    row = lax.broadcasted_iota(jnp.int32, (c_sz, c_sz), 0)
    col = lax.broadcasted_iota(jnp.int32, (c_sz, c_sz), 1)
    causal = col <= row
    tril = jnp.where(causal, 1.0, 0.0).astype(BF16)
    go = go_ref[...]

    def chunk(c, carry):
        sl = pl.ds(pl.multiple_of(c * c_sz, c_sz), c_sz)
        la = la_ref[sl, :]
        la1 = la.astype(BF16)
        rem = la - la1.astype(F32)
        la2 = rem.astype(BF16)
        la3 = (rem - la2.astype(F32)).astype(BF16)
        b_all = _dot(tril, la1) + _dot(tril, la2) + _dot(tril, la3)
        for h in range(GLA_HEADS):
            kc = slice(h * GLA_DK, (h + 1) * GLA_DK)
            vc = slice(h * GLA_DV, (h + 1) * GLA_DV)
            b = b_all[:, kc]
            b_last = b[c_sz - 1:c_sz, :]
            q = q_ref[sl, kc].astype(F32) * (GLA_DK ** -0.5)
            k = k_ref[sl, kc].astype(F32)
            v = v_ref[sl, vc]
            q_dec = (q * jnp.exp(b)).astype(BF16)
            k_inv = (k * jnp.exp(-b)).astype(BF16)
            k_dec = (k * jnp.exp(b_last - b)).astype(BF16)
            attn = jnp.where(causal, _dot_nt(q_dec, k_inv), 0.0)
            st = st_ref[h]
            o = _dot(attn.astype(BF16), v) + _dot_nt(q_dec, st.astype(BF16))
            st_ref[h] = st * jnp.exp(b_last) + _dot_tn(v, k_dec)
            r = r_ref[sl, vc].astype(F32)
            o_ref[sl, vc] = (_rms(o, go) * (r * jax.nn.sigmoid(r))).astype(BF16)
        return carry

    lax.fori_loop(0, T_GLA // c_sz, chunk, 0)


def _gla_rec(proj, log_a, g_o):
    bsz, seq, _ = proj.shape
    v_blk = 2 * GLA_QK // GLA_V
    return pl.pallas_call(
        _gla_rec_kernel,
        grid=(bsz, seq // T_GLA),
        in_specs=[
            pl.BlockSpec((None, T_GLA, GLA_QK), lambda b, t: (b, t, 0)),
            pl.BlockSpec((None, T_GLA, GLA_QK), lambda b, t: (b, t, 1)),
            pl.BlockSpec((None, T_GLA, GLA_V), lambda b, t: (b, t, v_blk)),
            pl.BlockSpec((None, T_GLA, GLA_V), lambda b, t: (b, t, v_blk + 1)),
            pl.BlockSpec((None, T_GLA, GLA_QK), lambda b, t: (b, t, 0)),
            pl.BlockSpec((1, GLA_DV), lambda b, t: (0, 0)),
        ],
        out_specs=pl.BlockSpec((None, T_GLA, GLA_V), lambda b, t: (b, t, 0)),
        out_shape=jax.ShapeDtypeStruct((bsz, seq, GLA_V), BF16),
        scratch_shapes=[pltpu.VMEM((GLA_HEADS, GLA_DV, GLA_DK), F32)],
        compiler_params=_params("parallel", "arbitrary"),
        name="gla_rec",
    )(proj, proj, proj, proj, log_a, g_o)


def _matmul_res_kernel(a_ref, w_ref, res_ref, o_ref):
    o_ref[...] = res_ref[...] + _dot(a_ref[...], w_ref[...])


def _matmul_res(a, w, res):
    m, kdim = a.shape
    n = w.shape[1]
    return pl.pallas_call(
        _matmul_res_kernel,
        grid=(m // TM,),
        in_specs=[
            pl.BlockSpec((TM, kdim), lambda i: (i, 0)),
            pl.BlockSpec((kdim, n), lambda i: (0, 0)),
            pl.BlockSpec((TM, n), lambda i: (i, 0)),
        ],
        out_specs=pl.BlockSpec((TM, n), lambda i: (i, 0)),
        out_shape=jax.ShapeDtypeStruct((m, n), F32),
        compiler_params=_params("parallel"),
        name="matmul_res",
    )(a, w, res)


def _head_pair_rms(t, g2, low):
    sq = t * t
    s_all = jnp.sum(sq, axis=-1, keepdims=True)
    s_lo = jnp.sum(jnp.where(low, sq, 0.0), axis=-1, keepdims=True)
    ms = jnp.where(low, s_lo, s_all - s_lo) * (1.0 / HEAD_DIM)
    return t * lax.rsqrt(ms + EPS) * g2


def _q_proj_kernel(h_ref, g_ref, w_ref, gq_ref, o_ref):
    q = _dot(_rms(h_ref[...], g_ref[...]).astype(BF16), w_ref[...])
    low = _low_half((q.shape[0], PAIR_W))
    gq = gq_ref[...]
    for pr in range(N_PAIRS):
        sl = slice(pr * PAIR_W, (pr + 1) * PAIR_W)
        o_ref[:, sl] = _head_pair_rms(q[:, sl], gq, low).astype(BF16)


def _q_proj(h, g, w, gq2):
    m = h.shape[0]
    return pl.pallas_call(
        _q_proj_kernel,
        grid=(m // TM,),
        in_specs=[
            pl.BlockSpec((TM, D_MODEL), lambda i: (i, 0)),
            pl.BlockSpec((1, D_MODEL), lambda i: (0, 0)),
            pl.BlockSpec((D_MODEL, D_MODEL), lambda i: (0, 0)),
            pl.BlockSpec((1, PAIR_W), lambda i: (0, 0)),
        ],
        out_specs=pl.BlockSpec((TM, D_MODEL), lambda i: (i, 0)),
        out_shape=jax.ShapeDtypeStruct((m, D_MODEL), BF16),
        compiler_params=_params("parallel"),
        name="q_proj",
    )(h, g, w, gq2)


def _kv_kernel(h_ref, g_ref, w_ref, gk_ref, klo_ref, khi_ref, ve_ref, vo_ref):
    kv = _dot(_rms(h_ref[...], g_ref[...]).astype(BF16), w_ref[...])
    gk = gk_ref[...]
    ones = jnp.ones((kv.shape[0], HEAD_DIM), F32)
    zeros = jnp.zeros((kv.shape[0], HEAD_DIM), F32)
    klo, khi, ve, vo = [], [], [], []
    for i in range(N_KV_HEADS):
        k = _rms(kv[:, i * HEAD_DIM:(i + 1) * HEAD_DIM], gk)
        v = kv[:, KV_W + i * HEAD_DIM:KV_W + (i + 1) * HEAD_DIM]
        klo += [k, zeros]
        khi += [zeros, k]
        ve += [v, ones]
        vo += [ones, v]
    klo_ref[...] = jnp.concatenate(klo, axis=1).astype(BF16)
    khi_ref[...] = jnp.concatenate(khi, axis=1).astype(BF16)
    ve_ref[...] = jnp.concatenate(ve, axis=1).astype(BF16)
    vo_ref[...] = jnp.concatenate(vo, axis=1).astype(BF16)


def _shared_kv(h, g, w, gk):
    m = h.shape[0]
    wide = N_KV_HEADS * PAIR_W
    out = jax.ShapeDtypeStruct((m, wide), BF16)
    return pl.pallas_call(
        _kv_kernel,
        grid=(m // TM,),
        in_specs=[
            pl.BlockSpec((TM, D_MODEL), lambda i: (i, 0)),
            pl.BlockSpec((1, D_MODEL), lambda i: (0, 0)),
            pl.BlockSpec((D_MODEL, 2 * KV_W), lambda i: (0, 0)),
            pl.BlockSpec((1, HEAD_DIM), lambda i: (0, 0)),
        ],
        out_specs=[pl.BlockSpec((TM, wide), lambda i: (i, 0))] * 4,
        out_shape=[out] * 4,
        compiler_params=_params("parallel"),
        name="shared_kv",
    )(h, g, w, gk)


def _slot_head(kv, slot):
    return kv * GROUP + 2 * (slot % GROUP_PAIRS) + slot // GROUP_PAIRS


def _bias_kernel(rb_ref, bucket_ref, o_ref):
    later = pl.program_id(0)
    h = _slot_head(pl.program_id(1), pl.program_id(2))
    bucket = bucket_ref[...]
    acc = jnp.zeros(bucket.shape, F32)
    for b in range(N_BUCKETS):
        acc = jnp.where(bucket == b, rb_ref[b, h], acc)
    qi = lax.broadcasted_iota(jnp.int32, bucket.shape, 0)
    sj = lax.broadcasted_iota(jnp.int32, bucket.shape, 1)
    dist = qi + BLOCK - sj
    kpos = later * BLOCK - BLOCK + sj
    mask = (dist >= 0) & (dist < WINDOW) & (kpos >= 0)
    o_ref[...] = jnp.where(mask, acc, -jnp.inf)


def _bias_table(rel_bias, bucket):
    return pl.pallas_call(
        _bias_kernel,
        grid=(2, N_KV_HEADS, GROUP),
        in_specs=[
            pl.BlockSpec(memory_space=pltpu.SMEM),
            pl.BlockSpec((BLOCK, 2 * BLOCK), lambda f, k, s: (0, 0)),
        ],
        out_specs=pl.BlockSpec((None, None, BLOCK, 2 * BLOCK), lambda f, k, s: (f, k, s, 0)),
        out_shape=jax.ShapeDtypeStruct((2, N_KV_HEADS, GROUP_ROWS, 2 * BLOCK), F32),
        compiler_params=_params("parallel", "parallel", "parallel"),
        name="bias_table",
    )(rel_bias, bucket)


def _swa_kernel(q_ref, klp_ref, klc_ref, khp_ref, khc_ref, vep_ref, vec_ref, vop_ref, voc_ref,
                bias_ref, sink_ref, o_ref):
    low = _low_half((BLOCK, PAIR_W))
    half = GROUP_ROWS // 2

    def softmax_numerator(s, sink):
        mx = jnp.maximum(jnp.max(s, axis=-1, keepdims=True), sink)
        return jnp.exp(s - mx).astype(BF16), jnp.exp(sink - mx)

    for kv in range(N_KV_HEADS):
        lanes = slice(kv * PAIR_W, (kv + 1) * PAIR_W)
        band = lambda prev, cur: jnp.concatenate([prev[:, lanes], cur[:, lanes]], axis=0)
        klo, khi = band(klp_ref, klc_ref), band(khp_ref, khc_ref)
        ve, vo = band(vep_ref, vec_ref), band(vop_ref, voc_ref)
        q0 = kv * GROUP_PAIRS * PAIR_W
        qs = jnp.concatenate([q_ref[:, q0 + pr * PAIR_W:q0 + (pr + 1) * PAIR_W]
                              for pr in range(GROUP_PAIRS)], axis=0)
        pe, ste = softmax_numerator(_dot_nt(qs, klo) + bias_ref[kv, :half], sink_ref[kv, :half])
        po, sto = softmax_numerator(_dot_nt(qs, khi) + bias_ref[kv, half:], sink_ref[kv, half:])
        oe = _dot(pe, ve)
        oo = _dot(po, vo)
        for pr in range(GROUP_PAIRS):
            rows = slice(pr * BLOCK, (pr + 1) * BLOCK)
            te, to = oe[rows], oo[rows]
            num = jnp.where(low, te, to)
            den = pltpu.roll(jnp.where(low, to, te), HEAD_DIM, 1)
            st = jnp.where(low, ste[rows], sto[rows])
            o_ref[:, q0 + pr * PAIR_W:q0 + (pr + 1) * PAIR_W] = (num / (den + st)).astype(BF16)


def _swa_attn(q, klo, khi, ve, vo, bias, sink_rows):
    bsz, seq, _ = q.shape
    nb = seq // BLOCK
    wide = N_KV_HEADS * PAIR_W
    prev = lambda b, n: (b, jnp.maximum(n - 1, 0), 0)
    cur = lambda b, n: (b, n, 0)
    kv_prev = pl.BlockSpec((None, BLOCK, wide), prev)
    kv_cur = pl.BlockSpec((None, BLOCK, wide), cur)
    return pl.pallas_call(
        _swa_kernel,
        grid=(bsz, nb),
        in_specs=[
            pl.BlockSpec((None, BLOCK, D_MODEL), cur),
            kv_prev, kv_cur, kv_prev, kv_cur, kv_prev, kv_cur, kv_prev, kv_cur,
            pl.BlockSpec((None, N_KV_HEADS, GROUP_ROWS, 2 * BLOCK),
                         lambda b, n: (jnp.minimum(n, 1), 0, 0, 0)),
            pl.BlockSpec((N_KV_HEADS, GROUP_ROWS, 1), lambda b, n: (0, 0, 0)),
        ],
        out_specs=pl.BlockSpec((None, BLOCK, D_MODEL), cur),
        out_shape=jax.ShapeDtypeStruct((bsz, seq, D_MODEL), BF16),
        compiler_params=_params("parallel", "arbitrary"),
        name="swa_attn",
    )(q, klo, klo, khi, khi, ve, ve, vo, vo, bias, sink_rows)


def _ple_kernel(h_ref, g_ref, wgate_ref, p_ref, wple_ref, o_ref):
    x = h_ref[...]
    gate = jax.nn.sigmoid(_dot(_rms(x, g_ref[...]).astype(BF16), wgate_ref[...]))
    o_ref[...] = x + gate * _dot(p_ref[...].astype(BF16), wple_ref[...])


def _ple(h, g, w_gate, p, w_ple):
    m = h.shape[0]
    return pl.pallas_call(
        _ple_kernel,
        grid=(m // TM,),
        in_specs=[
            pl.BlockSpec((TM, D_MODEL), lambda i: (i, 0)),
            pl.BlockSpec((1, D_MODEL), lambda i: (0, 0)),
            pl.BlockSpec((D_MODEL, D_MODEL), lambda i: (0, 0)),
            pl.BlockSpec((TM, PLE_DIM), lambda i: (i, 0)),
            pl.BlockSpec((PLE_DIM, D_MODEL), lambda i: (0, 0)),
        ],
        out_specs=pl.BlockSpec((TM, D_MODEL), lambda i: (i, 0)),
        out_shape=jax.ShapeDtypeStruct((m, D_MODEL), F32),
        compiler_params=_params("parallel"),
        name="ple",
    )(h, g, w_gate, p, w_ple)


def _t5_bucket(d):
    max_exact = N_BUCKETS // 2
    is_small = d < max_exact
    df = jnp.maximum(d, 1).astype(F32)
    large = max_exact + (jnp.log(df / max_exact) / math.log(MAX_DISTANCE / max_exact)
                         * (N_BUCKETS - max_exact)).astype(jnp.int32)
    large = jnp.minimum(large, N_BUCKETS - 1)
    return jnp.where(is_small, d, large)


def _sink_rows(sinks):
    per_slot = sinks.astype(F32).reshape(N_KV_HEADS, GROUP_PAIRS, 2).transpose(0, 2, 1)
    return jnp.repeat(per_slot.reshape(N_KV_HEADS, GROUP), BLOCK, axis=1)[..., None]


def kernel(x, p, norm_g, ffn_wg, ffn_wu, ffn_wd, ple_w, ple_gate_w, gla_w_in, gla_w_a2, gla_b_a,
           gla_norm_g, gla_w_o, kv_norm_g, w_kv, k_norm_g, rel_bias, swa_w_q, q_norm_g, sinks, swa_w_o):
    bsz, seq, d = x.shape
    depth = norm_g.shape[0]
    n_a = gla_w_in.shape[0]
    m = bsz * seq
    row = lambda v: v.reshape(1, -1)

    qi = jnp.arange(BLOCK)[:, None]
    sj = jnp.arange(2 * BLOCK)[None, :]
    bucket = _t5_bucket(jnp.maximum(qi + BLOCK - sj, 0)).astype(jnp.int32)
    bias = _bias_table(rel_bias, bucket)

    h = x.reshape(m, d)
    kv_tiles = None
    w_in_all = gla_w_in.astype(BF16)
    ffn_w = (ffn_wg[0, 0].astype(BF16), ffn_wu[0, 0].astype(BF16), ffn_wd[0, 0].astype(BF16))

    def ffn(h, g, layer, sub):
        nxt = (layer, sub + 1) if sub == 0 else (layer + 1, 0)
        next_weights = (ffn_wg, ffn_wu, ffn_wd) + nxt if nxt[0] < depth else None
        return _ffn(h, g, *ffn_w, next_weights=next_weights)

    for i in range(depth):
        if i == n_a:
            kv_tiles = [t.reshape(bsz, seq, -1) for t in
                        _shared_kv(h, row(kv_norm_g), w_kv.astype(BF16), row(k_norm_g))]
        h, ffn_w = ffn(h, row(norm_g[i, 0]), i, 0)
        if i < n_a:
            w_a1 = jnp.pad(gla_w_in[i][:, GLA_MAIN:], ((0, 0), (0, V7X_LANES - GLA_RANK))).astype(BF16)
            w_a2 = jnp.pad(gla_w_a2[i], ((0, V7X_LANES - GLA_RANK), (0, 0))).astype(BF16)
            proj, log_a = _gla_proj(h, row(norm_g[i, 1]), w_in_all, i, w_a1, w_a2, row(gla_b_a[i]))
            og = _gla_rec(proj.reshape(bsz, seq, GLA_MAIN), log_a.reshape(bsz, seq, GLA_QK),
                          row(gla_norm_g[i]))
            h = _matmul_res(og.reshape(m, GLA_V), gla_w_o[i].astype(BF16), h)
        else:
            j = i - n_a
            gq2 = row(jnp.tile(q_norm_g[j], 2)) * (HEAD_DIM ** -0.5)
            q = _q_proj(h, row(norm_g[i, 1]), swa_w_q[j].astype(BF16), gq2)
            o = _swa_attn(q.reshape(bsz, seq, d), *kv_tiles, bias, _sink_rows(sinks[j]))
            h = _matmul_res(o.reshape(m, d), swa_w_o[j].astype(BF16), h)
        h, ffn_w = ffn(h, row(norm_g[i, 2]), i, 1)
        h = _ple(h, row(norm_g[i, 3]), ple_gate_w[i].astype(BF16), p[i].reshape(m, PLE_DIM),
                 ple_w[i].astype(BF16))
    return h.reshape(bsz, seq, d)
```

```python
import functools
import math

import jax
import jax.numpy as jnp
from jax import lax
from jax.experimental import pallas as pl
from jax.experimental.pallas import tpu as pltpu

F32 = jnp.float32
BF16 = jnp.bfloat16

D_MODEL = 2048
D_FF = 5632
FFN_RES = 0.5
PLE_DIM = 256
EPS = 1e-6
GLA_HEADS = 4
GLA_QK = 1024
GLA_V = 2048
GLA_DK = 256
GLA_DV = 512
GLA_RANK = 16
GLA_GATE_NORM = 16.0
GLA_CHUNK = 64
GLA_MAIN = 2 * GLA_QK + 2 * GLA_V
HEAD_DIM = 64
N_Q_HEADS = 32
N_KV_HEADS = 4
GROUP = 8
KV_W = N_KV_HEADS * HEAD_DIM
WINDOW = 128
BLOCK = WINDOW
N_BUCKETS = 32
MAX_DISTANCE = WINDOW

V7X_LANES = 128
V7X_VMEM_BYTES = 64 * 1024 * 1024
VMEM_LIMIT = V7X_VMEM_BYTES - 8 * 1024 * 1024

PAIR_W = 2 * HEAD_DIM
assert PAIR_W == V7X_LANES
N_PAIRS = N_Q_HEADS // 2
GROUP_PAIRS = GROUP // 2

TM = 512
TM_WIDE = 1024
TM_FFN = 1024
TF = 512
TM_PROJ = 1024
TN_PROJ = 1536
GATE_COLS = GLA_QK // (GLA_MAIN // TN_PROJ)
T_GLA = 512


def _params(*sem):
    return pltpu.CompilerParams(dimension_semantics=sem, vmem_limit_bytes=VMEM_LIMIT)


def _rms(x, g):
    return x * lax.rsqrt(jnp.mean(x * x, axis=-1, keepdims=True) + EPS) * g


def _dot(a, b):
    return jnp.dot(a, b, preferred_element_type=F32)


def _dot_nt(a, b):
    return lax.dot_general(a, b, (((1,), (1,)), ((), ())), preferred_element_type=F32)


def _dot_tn(a, b):
    return lax.dot_general(a, b, (((0,), (0,)), ((), ())), preferred_element_type=F32)


def _ffn_kernel(h_hbm, g_ref, wg_ref, wu_ref, wd_ref, *rest, cast_next):
    if cast_next:
        nwg_ref, nwu_ref, nwd_ref, o_ref, nwg_out, nwu_out, nwd_out, x_ref, xn_ref, x_sem = rest
        nwg_out[...] = nwg_ref[...].astype(BF16)
        nwu_out[...] = nwu_ref[...].astype(BF16)
        nwd_out[...] = nwd_ref[...].astype(BF16)
    else:
        o_ref, x_ref, xn_ref, x_sem = rest
    i = pl.program_id(0)
    j = pl.program_id(1)

    def x_copy(tile):
        rows = pl.ds(pl.multiple_of(tile * TM_FFN, TM_FFN), TM_FFN)
        return pltpu.make_async_copy(h_hbm.at[rows], x_ref, x_sem)

    @pl.when(j == 0)
    def _():
        @pl.when(i == 0)
        def _():
            x_copy(0).start()

        x_copy(i).wait()
        x = x_ref[...]
        xn_ref[...] = _rms(x, g_ref[...]).astype(BF16)
        o_ref[...] = x

    @pl.when((j == 1) & (i + 1 < pl.num_programs(0)))
    def _():
        x_copy(i + 1).start()

    xn = xn_ref[...]
    gate = _dot(xn, wg_ref[...])
    up = _dot(xn, wu_ref[...])
    act = (gate * jax.nn.sigmoid(gate) * up * FFN_RES).astype(BF16)
    o_ref[...] += _dot(act, wd_ref[...])


def _ffn(h, g, wg, wu, wd, next_weights=None):
    m = h.shape[0]
    n_i, n_j = m // TM_FFN, D_FF // TF
    in_specs = [
        pl.BlockSpec(memory_space=pl.ANY),
        pl.BlockSpec((1, D_MODEL), lambda i, j: (0, 0)),
        pl.BlockSpec((D_MODEL, TF), lambda i, j: (0, j)),
        pl.BlockSpec((D_MODEL, TF), lambda i, j: (0, j)),
        pl.BlockSpec((TF, D_MODEL), lambda i, j: (j, 0)),
    ]
    out_specs = [pl.BlockSpec((TM_FFN, D_MODEL), lambda i, j: (i, 0))]
    out_shape = [jax.ShapeDtypeStruct((m, D_MODEL), F32)]
    args = [h, g, wg, wu, wd]
    if next_weights is not None:
        nwg, nwu, nwd, layer, sub = next_weights
        rows = D_MODEL // n_i
        in_specs += [
            pl.BlockSpec((None, None, rows, TF), lambda i, j: (layer, sub, i, j)),
            pl.BlockSpec((None, None, rows, TF), lambda i, j: (layer, sub, i, j)),
            pl.BlockSpec((None, None, TF, rows), lambda i, j: (layer, sub, j, i)),
        ]
        out_specs += [
            pl.BlockSpec((rows, TF), lambda i, j: (i, j)),
            pl.BlockSpec((rows, TF), lambda i, j: (i, j)),
            pl.BlockSpec((TF, rows), lambda i, j: (j, i)),
        ]
        out_shape += [
            jax.ShapeDtypeStruct((D_MODEL, D_FF), BF16),
            jax.ShapeDtypeStruct((D_MODEL, D_FF), BF16),
            jax.ShapeDtypeStruct((D_FF, D_MODEL), BF16),
        ]
        args += [nwg, nwu, nwd]
    outs = pl.pallas_call(
        functools.partial(_ffn_kernel, cast_next=next_weights is not None),
        grid=(n_i, n_j),
        in_specs=in_specs,
        out_specs=out_specs,
        out_shape=out_shape,
        scratch_shapes=[
            pltpu.VMEM((TM_FFN, D_MODEL), F32),
            pltpu.VMEM((TM_FFN, D_MODEL), BF16),
            pltpu.SemaphoreType.DMA,
        ],
        compiler_params=_params("arbitrary", "arbitrary"),
        name="ffn",
    )(*args)
    return outs[0], tuple(outs[1:])


def _gla_proj_kernel(h_ref, g_ref, w_ref, wa1_ref, wa2_ref, ba_ref, proj_ref, la_ref, xn_ref, a1_ref):
    @pl.when(pl.program_id(1) == 0)
    def _():
        xn = _rms(h_ref[...], g_ref[...]).astype(BF16)
        xn_ref[...] = xn
        a1_ref[...] = _dot(xn, wa1_ref[...]).astype(BF16)

    z = _dot(a1_ref[...], wa2_ref[...]) + ba_ref[...]
    log_sig = jnp.minimum(z, 0.0) - jnp.log1p(jnp.exp(-jnp.abs(z)))
    la_ref[...] = log_sig * (1.0 / GLA_GATE_NORM)
    proj_ref[...] = _dot(xn_ref[...], w_ref[...]).astype(BF16)


def _gla_proj(h, g, w_in_all, layer, w_a1, w_a2, b_a):
    m = h.shape[0]
    return pl.pallas_call(
        _gla_proj_kernel,
        grid=(m // TM_PROJ, GLA_MAIN // TN_PROJ),
        in_specs=[
            pl.BlockSpec((TM_PROJ, D_MODEL), lambda i, j: (i, 0)),
            pl.BlockSpec((1, D_MODEL), lambda i, j: (0, 0)),
            pl.BlockSpec((None, D_MODEL, TN_PROJ), lambda i, j: (layer, 0, j)),
            pl.BlockSpec((D_MODEL, V7X_LANES), lambda i, j: (0, 0)),
            pl.BlockSpec((V7X_LANES, GATE_COLS), lambda i, j: (0, j)),
            pl.BlockSpec((1, GATE_COLS), lambda i, j: (0, j)),
        ],
        out_specs=[
            pl.BlockSpec((TM_PROJ, TN_PROJ), lambda i, j: (i, j)),
            pl.BlockSpec((TM_PROJ, GATE_COLS), lambda i, j: (i, j)),
        ],
        out_shape=[
            jax.ShapeDtypeStruct((m, GLA_MAIN), BF16),
            jax.ShapeDtypeStruct((m, GLA_QK), F32),
        ],
        scratch_shapes=[pltpu.VMEM((TM_PROJ, D_MODEL), BF16), pltpu.VMEM((TM_PROJ, V7X_LANES), BF16)],
        compiler_params=_params("parallel", "arbitrary"),
        name="gla_proj",
    )(h, g, w_in_all, w_a1, w_a2, b_a)


def _gla_rec_kernel(q_ref, k_ref, v_ref, r_ref, la_ref, go_ref, o_ref, st_ref):
    c_sz = GLA_CHUNK

    @pl.when(pl.program_id(1) == 0)
    def _():
        st_ref[...] = jnp.zeros_like(st_ref)

    row = lax.broadcasted_iota(jnp.int32, (c_sz, c_sz), 0)
    col = lax.broadcasted_iota(jnp.int32, (c_sz, c_sz), 1)
    causal = col <= row
    tril = jnp.where(causal, 1.0, 0.0).astype(BF16)
    go = go_ref[...]

    def chunk(c, carry):
        sl = pl.ds(pl.multiple_of(c * c_sz, c_sz), c_sz)
        la = la_ref[sl, :]
        la1 = la.astype(BF16)
        rem = la - la1.astype(F32)
        la2 = rem.astype(BF16)
        la3 = (rem - la2.astype(F32)).astype(BF16)
        b_all = _dot(tril, la1) + _dot(tril, la2) + _dot(tril, la3)
        for h in range(GLA_HEADS):
            kc = slice(h * GLA_DK, (h + 1) * GLA_DK)
            vc = slice(h * GLA_DV, (h + 1) * GLA_DV)
            b = b_all[:, kc]
            b_last = b[c_sz - 1:c_sz, :]
            q = q_ref[sl, kc].astype(F32) * (GLA_DK ** -0.5)
            k = k_ref[sl, kc].astype(F32)
            v = v_ref[sl, vc]
            q_dec = (q * jnp.exp(b)).astype(BF16)
            k_inv = (k * jnp.exp(-b)).astype(BF16)
            k_dec = (k * jnp.exp(b_last - b)).astype(BF16)
            attn = jnp.where(causal, _dot_nt(q_dec, k_inv), 0.0)
            st = st_ref[h]
            o = _dot(attn.astype(BF16), v) + _dot_nt(q_dec, st.astype(BF16))
            st_ref[h] = st * jnp.exp(b_last) + _dot_tn(v, k_dec)
            r = r_ref[sl, vc].astype(F32)
            o_ref[sl, vc] = (_rms(o, go) * (r * jax.nn.sigmoid(r))).astype(BF16)
        return carry

    lax.fori_loop(0, T_GLA // c_sz, chunk, 0)


def _gla_rec(proj, log_a, g_o):
    bsz, seq, _ = proj.shape
    v_blk = 2 * GLA_QK // GLA_V
    return pl.pallas_call(
        _gla_rec_kernel,
        grid=(bsz, seq // T_GLA),
        in_specs=[
            pl.BlockSpec((None, T_GLA, GLA_QK), lambda b, t: (b, t, 0)),
            pl.BlockSpec((None, T_GLA, GLA_QK), lambda b, t: (b, t, 1)),
            pl.BlockSpec((None, T_GLA, GLA_V), lambda b, t: (b, t, v_blk)),
            pl.BlockSpec((None, T_GLA, GLA_V), lambda b, t: (b, t, v_blk + 1)),
            pl.BlockSpec((None, T_GLA, GLA_QK), lambda b, t: (b, t, 0)),
            pl.BlockSpec((1, GLA_DV), lambda b, t: (0, 0)),
        ],
        out_specs=pl.BlockSpec((None, T_GLA, GLA_V), lambda b, t: (b, t, 0)),
        out_shape=jax.ShapeDtypeStruct((bsz, seq, GLA_V), BF16),
        scratch_shapes=[pltpu.VMEM((GLA_HEADS, GLA_DV, GLA_DK), F32)],
        compiler_params=_params("parallel", "arbitrary"),
        name="gla_rec",
    )(proj, proj, proj, proj, log_a, g_o)


def _matmul_res_kernel(a_ref, w_ref, res_ref, o_ref):
    o_ref[...] = res_ref[...] + _dot(a_ref[...], w_ref[...])


def _matmul_res(a, w, res):
    m, kdim = a.shape
    n = w.shape[1]
    return pl.pallas_call(
        _matmul_res_kernel,
        grid=(m // TM_WIDE,),
        in_specs=[
            pl.BlockSpec((TM_WIDE, kdim), lambda i: (i, 0)),
            pl.BlockSpec((kdim, n), lambda i: (0, 0), pipeline_mode=pl.Buffered(1)),
            pl.BlockSpec((TM_WIDE, n), lambda i: (i, 0)),
        ],
        out_specs=pl.BlockSpec((TM_WIDE, n), lambda i: (i, 0)),
        out_shape=jax.ShapeDtypeStruct((m, n), F32),
        compiler_params=_params("parallel"),
        name="matmul_res",
    )(a, w, res)


def _q_proj_kernel(h_ref, g_ref, w_ref, gq_ref, o_ref):
    q = _dot(_rms(h_ref[...], g_ref[...]).astype(BF16), w_ref[...])
    tm = q.shape[0]
    gq = jnp.broadcast_to(gq_ref[...], (PAIR_W, tm))
    for pr in range(N_PAIRS):
        t = q[:, pr * PAIR_W:(pr + 1) * PAIR_W].T
        halves = []
        for hd in range(2):
            th = t[hd * HEAD_DIM:(hd + 1) * HEAD_DIM]
            ms = jnp.mean(th * th, axis=0, keepdims=True)
            halves.append(th * lax.rsqrt(ms + EPS))
        o_ref[pr * PAIR_W:(pr + 1) * PAIR_W, :] = (jnp.concatenate(halves, axis=0) * gq).astype(BF16)


def _q_proj(h, g, w, gq_col):
    m = h.shape[0]
    return pl.pallas_call(
        _q_proj_kernel,
        grid=(m // TM,),
        in_specs=[
            pl.BlockSpec((TM, D_MODEL), lambda i: (i, 0)),
            pl.BlockSpec((1, D_MODEL), lambda i: (0, 0)),
            pl.BlockSpec((D_MODEL, D_MODEL), lambda i: (0, 0)),
            pl.BlockSpec((PAIR_W, 1), lambda i: (0, 0)),
        ],
        out_specs=pl.BlockSpec((D_MODEL, TM), lambda i: (0, i)),
        out_shape=jax.ShapeDtypeStruct((D_MODEL, m), BF16),
        compiler_params=_params("parallel"),
        name="q_proj",
    )(h, g, w, gq_col)


def _kv_kernel(h_ref, g_ref, w_ref, gk_ref, klo_ref, khi_ref, ve_ref, vo_ref):
    kv = _dot(_rms(h_ref[...], g_ref[...]).astype(BF16), w_ref[...])
    tm = kv.shape[0]
    gk = gk_ref[...]
    zeros = jnp.zeros((tm, HEAD_DIM), F32)
    ones_t = jnp.ones((HEAD_DIM, tm), F32)
    klo, khi = [], []
    for i in range(N_KV_HEADS):
        k = _rms(kv[:, i * HEAD_DIM:(i + 1) * HEAD_DIM], gk)
        klo += [k, zeros]
        khi += [zeros, k]
    klo_ref[...] = jnp.concatenate(klo, axis=1).astype(BF16)
    khi_ref[...] = jnp.concatenate(khi, axis=1).astype(BF16)
    for i in range(N_KV_HEADS // 2):
        vt = kv[:, KV_W + i * PAIR_W:KV_W + (i + 1) * PAIR_W].T
        for hd in range(2):
            v_t = vt[hd * HEAD_DIM:(hd + 1) * HEAD_DIM]
            rows = slice((2 * i + hd) * PAIR_W, (2 * i + hd + 1) * PAIR_W)
            ve_ref[rows, :] = jnp.concatenate([v_t, ones_t], axis=0).astype(BF16)
            vo_ref[rows, :] = jnp.concatenate([ones_t, v_t], axis=0).astype(BF16)


def _shared_kv(h, g, w, gk):
    m = h.shape[0]
    wide = N_KV_HEADS * PAIR_W
    return pl.pallas_call(
        _kv_kernel,
        grid=(m // TM,),
        in_specs=[
            pl.BlockSpec((TM, D_MODEL), lambda i: (i, 0)),
            pl.BlockSpec((1, D_MODEL), lambda i: (0, 0)),
            pl.BlockSpec((D_MODEL, 2 * KV_W), lambda i: (0, 0)),
            pl.BlockSpec((1, HEAD_DIM), lambda i: (0, 0)),
        ],
        out_specs=[pl.BlockSpec((TM, wide), lambda i: (i, 0))] * 2
        + [pl.BlockSpec((wide, TM), lambda i: (0, i))] * 2,
        out_shape=[jax.ShapeDtypeStruct((m, wide), BF16)] * 2 + [jax.ShapeDtypeStruct((wide, m), BF16)] * 2,
        compiler_params=_params("parallel"),
        name="shared_kv",
    )(h, g, w, gk)


def _bias_kernel(rb_ref, sinks_ref, bucket_ref, o_ref):
    layer, later, kv, parity, pair = (pl.program_id(a) for a in range(5))
    h = kv * GROUP + 2 * pair + parity
    bucket = bucket_ref[...]
    acc = jnp.zeros(bucket.shape, F32)
    for b in range(N_BUCKETS):
        acc = jnp.where(bucket == b, rb_ref[b, h], acc)
    sj = lax.broadcasted_iota(jnp.int32, bucket.shape, 0)
    qi = lax.broadcasted_iota(jnp.int32, bucket.shape, 1)
    dist = qi + BLOCK - sj
    kpos = later * BLOCK - BLOCK + sj
    mask = (dist >= 0) & (dist < WINDOW) & (kpos >= 0)
    o_ref[...] = jnp.where(sj == 0, sinks_ref[layer, h], jnp.where(mask, acc, -jnp.inf))


def _bias_table(rel_bias, sinks, bucket_t):
    n_layers = sinks.shape[0]
    smem = pl.BlockSpec(memory_space=pltpu.SMEM)
    return pl.pallas_call(
        _bias_kernel,
        grid=(n_layers, 2, N_KV_HEADS, 2, GROUP_PAIRS),
        in_specs=[smem, smem, pl.BlockSpec((2 * BLOCK, BLOCK), lambda l, f, k, e, p: (0, 0))],
        out_specs=pl.BlockSpec((None, None, None, None, 2 * BLOCK, BLOCK),
                               lambda l, f, k, e, p: (l, f, k, e, 0, p)),
        out_shape=jax.ShapeDtypeStruct((n_layers, 2, N_KV_HEADS, 2, 2 * BLOCK, GROUP_PAIRS * BLOCK), F32),
        compiler_params=_params(*["parallel"] * 5),
        name="bias_table",
    )(rel_bias, sinks, bucket_t)


def _swa_kernel(q_ref, klp_ref, klc_ref, khp_ref, khc_ref, vep_ref, vec_ref, vop_ref, voc_ref,
                bias_ref, o_ref):
    k_keep = jnp.where(lax.broadcasted_iota(jnp.int32, (2 * BLOCK, PAIR_W), 0) == 0, 0.0, 1.0).astype(BF16)
    v_row = lax.broadcasted_iota(jnp.int32, (PAIR_W, 2 * BLOCK), 0)
    v_col = lax.broadcasted_iota(jnp.int32, (PAIR_W, 2 * BLOCK), 1)
    ve_keep = jnp.where((v_col == 0) & (v_row < HEAD_DIM), 0.0, 1.0).astype(BF16)
    vo_keep = jnp.where((v_col == 0) & (v_row >= HEAD_DIM), 0.0, 1.0).astype(BF16)

    for kv in range(N_KV_HEADS):
        tile = slice(kv * PAIR_W, (kv + 1) * PAIR_W)
        klo = jnp.concatenate([klp_ref[:, tile], klc_ref[:, tile]], axis=0) * k_keep
        khi = jnp.concatenate([khp_ref[:, tile], khc_ref[:, tile]], axis=0) * k_keep
        ve = jnp.concatenate([vep_ref[tile, :], vec_ref[tile, :]], axis=1) * ve_keep
        vo = jnp.concatenate([vop_ref[tile, :], voc_ref[tile, :]], axis=1) * vo_keep
        q0 = kv * GROUP_PAIRS * PAIR_W
        q_t = jnp.concatenate([q_ref[q0 + pr * PAIR_W:q0 + (pr + 1) * PAIR_W, :]
                               for pr in range(GROUP_PAIRS)], axis=1)

        def probs(k, parity):
            s = _dot(k, q_t) + bias_ref[kv, parity]
            return jnp.exp(s - jnp.max(s, axis=0, keepdims=True)).astype(BF16)

        oe = _dot(ve, probs(klo, 0))
        oo = _dot(vo, probs(khi, 1))
        num = jnp.concatenate([oe[:HEAD_DIM], oo[HEAD_DIM:]], axis=0)
        den = jnp.concatenate([oe[HEAD_DIM:], oo[:HEAD_DIM]], axis=0)
        out_t = num / den
        for pr in range(GROUP_PAIRS):
            o_ref[:, q0 + pr * PAIR_W:q0 + (pr + 1) * PAIR_W] = (
                out_t[:, pr * BLOCK:(pr + 1) * BLOCK].T.astype(BF16))


def _swa_attn(q_t, klo, khi, ve_t, vo_t, bias, layer):
    bsz, seq, wide = klo.shape
    nb = seq // BLOCK
    prev = lambda b, n: (b, jnp.maximum(n - 1, 0), 0)
    cur = lambda b, n: (b, n, 0)
    prev_t = lambda b, n: (0, b * nb + jnp.maximum(n - 1, 0))
    cur_t = lambda b, n: (0, b * nb + n)
    k_prev, k_cur = pl.BlockSpec((None, BLOCK, wide), prev), pl.BlockSpec((None, BLOCK, wide), cur)
    v_prev, v_cur = pl.BlockSpec((wide, BLOCK), prev_t), pl.BlockSpec((wide, BLOCK), cur_t)
    return pl.pallas_call(
        _swa_kernel,
        grid=(bsz, nb),
        in_specs=[
            pl.BlockSpec((D_MODEL, BLOCK), cur_t),
            k_prev, k_cur, k_prev, k_cur, v_prev, v_cur, v_prev, v_cur,
            pl.BlockSpec((None, None, N_KV_HEADS, 2, 2 * BLOCK, GROUP_PAIRS * BLOCK),
                         lambda b, n: (layer, jnp.minimum(n, 1), 0, 0, 0, 0)),
        ],
        out_specs=pl.BlockSpec((None, BLOCK, D_MODEL), cur),
        out_shape=jax.ShapeDtypeStruct((bsz, seq, D_MODEL), BF16),
        compiler_params=_params("parallel", "arbitrary"),
        name="swa_attn",
    )(q_t, klo, klo, khi, khi, ve_t, ve_t, vo_t, vo_t, bias)


def _ple_kernel(h_ref, g_ref, wgate_ref, p_ref, wple_ref, o_ref):
    x = h_ref[...]
    gate = jax.nn.sigmoid(_dot(_rms(x, g_ref[...]).astype(BF16), wgate_ref[...]))
    o_ref[...] = x + gate * _dot(p_ref[...].astype(BF16), wple_ref[...])


def _ple(h, g, w_gate_all, p_all, w_ple, layer):
    m = h.shape[0]
    return pl.pallas_call(
        _ple_kernel,
        grid=(m // TM_WIDE,),
        in_specs=[
            pl.BlockSpec((TM_WIDE, D_MODEL), lambda i: (i, 0)),
            pl.BlockSpec((1, D_MODEL), lambda i: (0, 0)),
            pl.BlockSpec((None, D_MODEL, D_MODEL), lambda i: (layer, 0, 0), pipeline_mode=pl.Buffered(1)),
            pl.BlockSpec((None, TM_WIDE, PLE_DIM), lambda i: (layer, i, 0)),
            pl.BlockSpec((PLE_DIM, D_MODEL), lambda i: (0, 0), pipeline_mode=pl.Buffered(1)),
        ],
        out_specs=pl.BlockSpec((TM_WIDE, D_MODEL), lambda i: (i, 0)),
        out_shape=jax.ShapeDtypeStruct((m, D_MODEL), F32),
        compiler_params=_params("parallel"),
        name="ple",
    )(h, g, w_gate_all, p_all, w_ple)


def _t5_bucket(d):
    max_exact = N_BUCKETS // 2
    is_small = d < max_exact
    df = jnp.maximum(d, 1).astype(F32)
    large = max_exact + (jnp.log(df / max_exact) / math.log(MAX_DISTANCE / max_exact)
                         * (N_BUCKETS - max_exact)).astype(jnp.int32)
    large = jnp.minimum(large, N_BUCKETS - 1)
    return jnp.where(is_small, d, large)


def kernel(x, p, norm_g, ffn_wg, ffn_wu, ffn_wd, ple_w, ple_gate_w, gla_w_in, gla_w_a2, gla_b_a,
           gla_norm_g, gla_w_o, kv_norm_g, w_kv, k_norm_g, rel_bias, swa_w_q, q_norm_g, sinks, swa_w_o):
    bsz, seq, d = x.shape
    depth = norm_g.shape[0]
    n_a = gla_w_in.shape[0]
    m = bsz * seq
    row = lambda v: v.reshape(1, -1)

    sj = jnp.arange(2 * BLOCK)[:, None]
    qi = jnp.arange(BLOCK)[None, :]
    bucket_t = _t5_bucket(jnp.maximum(qi + BLOCK - sj, 0)).astype(jnp.int32)
    bias = _bias_table(rel_bias, sinks.astype(F32), bucket_t)

    h = x.reshape(m, d)
    kv_tiles = None
    w_in_all = gla_w_in.astype(BF16)
    w_gate_all = ple_gate_w.astype(BF16)
    p_all = p.reshape(depth, m, PLE_DIM)
    ffn_w = (ffn_wg[0, 0].astype(BF16), ffn_wu[0, 0].astype(BF16), ffn_wd[0, 0].astype(BF16))

    def ffn(h, g, layer, sub):
        nxt = (layer, sub + 1) if sub == 0 else (layer + 1, 0)
        next_weights = (ffn_wg, ffn_wu, ffn_wd) + nxt if nxt[0] < depth else None
        return _ffn(h, g, *ffn_w, next_weights=next_weights)

    for i in range(depth):
        if i == n_a:
            klo, khi, ve_t, vo_t = _shared_kv(h, row(kv_norm_g), w_kv.astype(BF16), row(k_norm_g))
            kv_tiles = (klo.reshape(bsz, seq, -1), khi.reshape(bsz, seq, -1), ve_t, vo_t)
        h, ffn_w = ffn(h, row(norm_g[i, 0]), i, 0)
        if i < n_a:
            w_a1 = jnp.pad(gla_w_in[i][:, GLA_MAIN:], ((0, 0), (0, V7X_LANES - GLA_RANK))).astype(BF16)
            w_a2 = jnp.pad(gla_w_a2[i], ((0, V7X_LANES - GLA_RANK), (0, 0))).astype(BF16)
            proj, log_a = _gla_proj(h, row(norm_g[i, 1]), w_in_all, i, w_a1, w_a2, row(gla_b_a[i]))
            og = _gla_rec(proj.reshape(bsz, seq, GLA_MAIN), log_a.reshape(bsz, seq, GLA_QK),
                          row(gla_norm_g[i]))
            h = _matmul_res(og.reshape(m, GLA_V), gla_w_o[i].astype(BF16), h)
        else:
            j = i - n_a
            gq_col = (jnp.tile(q_norm_g[j], 2) * (HEAD_DIM ** -0.5)).reshape(PAIR_W, 1)
            q_t = _q_proj(h, row(norm_g[i, 1]), swa_w_q[j].astype(BF16), gq_col)
            o = _swa_attn(q_t, *kv_tiles, bias, j)
            h = _matmul_res(o.reshape(m, d), swa_w_o[j].astype(BF16), h)
        h, ffn_w = ffn(h, row(norm_g[i, 2]), i, 1)
        h = _ple(h, row(norm_g[i, 3]), w_gate_all, p_all, ple_w[i].astype(BF16), i)
    return h.reshape(bsz, seq, d)
```

```python
import functools
import math

import jax
import jax.numpy as jnp
from jax import lax
from jax.experimental import pallas as pl
from jax.experimental.pallas import tpu as pltpu

F32 = jnp.float32
BF16 = jnp.bfloat16

D_MODEL = 2048
D_FF = 5632
FFN_RES = 0.5
PLE_DIM = 256
EPS = 1e-6
GLA_HEADS = 4
GLA_QK = 1024
GLA_V = 2048
GLA_DK = 256
GLA_DV = 512
GLA_RANK = 16
GLA_GATE_NORM = 16.0
GLA_CHUNK = 64
GLA_MAIN = 2 * GLA_QK + 2 * GLA_V
HEAD_DIM = 64
N_Q_HEADS = 32
N_KV_HEADS = 4
GROUP = 8
KV_W = N_KV_HEADS * HEAD_DIM
WINDOW = 128
BLOCK = WINDOW
N_BUCKETS = 32
MAX_DISTANCE = WINDOW

V7X_LANES = 128
V7X_VMEM_BYTES = 64 * 1024 * 1024
VMEM_LIMIT = V7X_VMEM_BYTES - 8 * 1024 * 1024

PAIR_W = 2 * HEAD_DIM
assert PAIR_W == V7X_LANES
N_PAIRS = N_Q_HEADS // 2
GROUP_PAIRS = GROUP // 2

TM = 512
TM_WIDE = 1024
TM_FFN = 1024
TF = 512
TM_PROJ = 1024
TN_PROJ = 1536
GATE_COLS = GLA_QK // (GLA_MAIN // TN_PROJ)
T_GLA = 256


def _params(*sem):
    return pltpu.CompilerParams(dimension_semantics=sem, vmem_limit_bytes=VMEM_LIMIT)


def _rms(x, g):
    return x * lax.rsqrt(jnp.mean(x * x, axis=-1, keepdims=True) + EPS) * g


def _dot(a, b):
    return jnp.dot(a, b, preferred_element_type=F32)


def _dot_nt(a, b):
    return lax.dot_general(a, b, (((1,), (1,)), ((), ())), preferred_element_type=F32)


def _dot_tn(a, b):
    return lax.dot_general(a, b, (((0,), (0,)), ((), ())), preferred_element_type=F32)


def _ffn_kernel(h_hbm, g_ref, wg_ref, wu_ref, wd_ref, *rest, cast_next):
    if cast_next:
        nwg_ref, nwu_ref, nwd_ref, o_ref, nwg_out, nwu_out, nwd_out, x_ref, xn_ref, x_sem = rest
        nwg_out[...] = nwg_ref[...].astype(BF16)
        nwu_out[...] = nwu_ref[...].astype(BF16)
        nwd_out[...] = nwd_ref[...].astype(BF16)
    else:
        o_ref, x_ref, xn_ref, x_sem = rest
    i = pl.program_id(0)
    j = pl.program_id(1)

    def x_copy(tile):
        rows = pl.ds(pl.multiple_of(tile * TM_FFN, TM_FFN), TM_FFN)
        return pltpu.make_async_copy(h_hbm.at[rows], x_ref, x_sem)

    def swiglu_tile(xn):
        gate = _dot(xn, wg_ref[...])
        up = _dot(xn, wu_ref[...])
        act = (gate * jax.nn.sigmoid(gate) * up * FFN_RES).astype(BF16)
        return _dot(act, wd_ref[...])

    @pl.when(j == 0)
    def _():
        @pl.when(i == 0)
        def _():
            x_copy(0).start()

        x_copy(i).wait()
        x = x_ref[...]
        xn = _rms(x, g_ref[...]).astype(BF16)
        xn_ref[...] = xn
        o_ref[...] = x + swiglu_tile(xn)

    @pl.when(j > 0)
    def _():
        @pl.when((j == 1) & (i + 1 < pl.num_programs(0)))
        def _():
            x_copy(i + 1).start()

        o_ref[...] += swiglu_tile(xn_ref[...])


def _ffn(h, g, wg, wu, wd, next_weights=None):
    m = h.shape[0]
    n_i, n_j = m // TM_FFN, D_FF // TF
    in_specs = [
        pl.BlockSpec(memory_space=pl.ANY),
        pl.BlockSpec((1, D_MODEL), lambda i, j: (0, 0)),
        pl.BlockSpec((D_MODEL, TF), lambda i, j: (0, j)),
        pl.BlockSpec((D_MODEL, TF), lambda i, j: (0, j)),
        pl.BlockSpec((TF, D_MODEL), lambda i, j: (j, 0)),
    ]
    out_specs = [pl.BlockSpec((TM_FFN, D_MODEL), lambda i, j: (i, 0))]
    out_shape = [jax.ShapeDtypeStruct((m, D_MODEL), F32)]
    args = [h, g, wg, wu, wd]
    if next_weights is not None:
        nwg, nwu, nwd, layer, sub = next_weights
        rows = D_MODEL // n_i
        in_specs += [
            pl.BlockSpec((None, None, rows, TF), lambda i, j: (layer, sub, i, j)),
            pl.BlockSpec((None, None, rows, TF), lambda i, j: (layer, sub, i, j)),
            pl.BlockSpec((None, None, TF, rows), lambda i, j: (layer, sub, j, i)),
        ]
        out_specs += [
            pl.BlockSpec((rows, TF), lambda i, j: (i, j)),
            pl.BlockSpec((rows, TF), lambda i, j: (i, j)),
            pl.BlockSpec((TF, rows), lambda i, j: (j, i)),
        ]
        out_shape += [
            jax.ShapeDtypeStruct((D_MODEL, D_FF), BF16),
            jax.ShapeDtypeStruct((D_MODEL, D_FF), BF16),
            jax.ShapeDtypeStruct((D_FF, D_MODEL), BF16),
        ]
        args += [nwg, nwu, nwd]
    outs = pl.pallas_call(
        functools.partial(_ffn_kernel, cast_next=next_weights is not None),
        grid=(n_i, n_j),
        in_specs=in_specs,
        out_specs=out_specs,
        out_shape=out_shape,
        scratch_shapes=[
            pltpu.VMEM((TM_FFN, D_MODEL), F32),
            pltpu.VMEM((TM_FFN, D_MODEL), BF16),
            pltpu.SemaphoreType.DMA,
        ],
        compiler_params=_params("arbitrary", "arbitrary"),
        name="ffn",
    )(*args)
    return outs[0], tuple(outs[1:])


def _gla_proj_kernel(h_ref, g_ref, w_ref, wa1_ref, wa2_ref, ba_ref, proj_ref, la_ref, xn_ref, a1_ref):
    def column_step(xn, a1):
        z = _dot(a1, wa2_ref[...]) + ba_ref[...]
        log_sig = jnp.minimum(z, 0.0) - jnp.log1p(jnp.exp(-jnp.abs(z)))
        la_ref[...] = log_sig * (1.0 / GLA_GATE_NORM)
        proj_ref[...] = _dot(xn, w_ref[...]).astype(BF16)

    @pl.when(pl.program_id(1) == 0)
    def _():
        xn = _rms(h_ref[...], g_ref[...]).astype(BF16)
        a1 = _dot(xn, wa1_ref[...]).astype(BF16)
        xn_ref[...] = xn
        a1_ref[...] = a1
        column_step(xn, a1)

    @pl.when(pl.program_id(1) > 0)
    def _():
        column_step(xn_ref[...], a1_ref[...])


def _gla_proj(h, g, w_in_all, layer, w_a1, w_a2, b_a):
    m = h.shape[0]
    return pl.pallas_call(
        _gla_proj_kernel,
        grid=(m // TM_PROJ, GLA_MAIN // TN_PROJ),
        in_specs=[
            pl.BlockSpec((TM_PROJ, D_MODEL), lambda i, j: (i, 0)),
            pl.BlockSpec((1, D_MODEL), lambda i, j: (0, 0)),
            pl.BlockSpec((None, D_MODEL, TN_PROJ), lambda i, j: (layer, 0, j)),
            pl.BlockSpec((D_MODEL, V7X_LANES), lambda i, j: (0, 0)),
            pl.BlockSpec((V7X_LANES, GATE_COLS), lambda i, j: (0, j)),
            pl.BlockSpec((1, GATE_COLS), lambda i, j: (0, j)),
        ],
        out_specs=[
            pl.BlockSpec((TM_PROJ, TN_PROJ), lambda i, j: (i, j)),
            pl.BlockSpec((TM_PROJ, GATE_COLS), lambda i, j: (i, j)),
        ],
        out_shape=[
            jax.ShapeDtypeStruct((m, GLA_MAIN), BF16),
            jax.ShapeDtypeStruct((m, GLA_QK), F32),
        ],
        scratch_shapes=[pltpu.VMEM((TM_PROJ, D_MODEL), BF16), pltpu.VMEM((TM_PROJ, V7X_LANES), BF16)],
        compiler_params=_params("parallel", "arbitrary"),
        name="gla_proj",
    )(h, g, w_in_all, w_a1, w_a2, b_a)


def _gla_rec_kernel(q_ref, k_ref, v_ref, r_ref, la_ref, go_ref, o_ref, st_ref):
    c_sz = GLA_CHUNK
    bsz = q_ref.shape[0]

    @pl.when(pl.program_id(0) == 0)
    def _():
        st_ref[...] = jnp.zeros_like(st_ref)

    row = lax.broadcasted_iota(jnp.int32, (c_sz, c_sz), 0)
    col = lax.broadcasted_iota(jnp.int32, (c_sz, c_sz), 1)
    causal = col <= row
    tril = jnp.where(causal, 1.0, 0.0).astype(BF16)
    go = go_ref[...]

    def chunk(c, carry):
        sl = pl.ds(pl.multiple_of(c * c_sz, c_sz), c_sz)
        for bi in range(bsz):
            la = la_ref[bi, sl, :]
            la1 = la.astype(BF16)
            rem = la - la1.astype(F32)
            la2 = rem.astype(BF16)
            la3 = (rem - la2.astype(F32)).astype(BF16)
            b_all = _dot(tril, la1) + _dot(tril, la2) + _dot(tril, la3)
            for h in range(GLA_HEADS):
                kc = slice(h * GLA_DK, (h + 1) * GLA_DK)
                vc = slice(h * GLA_DV, (h + 1) * GLA_DV)
                b = b_all[:, kc]
                b_last = b[c_sz - 1:c_sz, :]
                q = q_ref[bi, sl, kc].astype(F32) * (GLA_DK ** -0.5)
                k = k_ref[bi, sl, kc].astype(F32)
                v = v_ref[bi, sl, vc]
                q_dec = (q * jnp.exp(b)).astype(BF16)
                k_inv = (k * jnp.exp(-b)).astype(BF16)
                k_dec = (k * jnp.exp(b_last - b)).astype(BF16)
                attn = jnp.where(causal, _dot_nt(q_dec, k_inv), 0.0)
                st = st_ref[bi, h]
                o = _dot(attn.astype(BF16), v) + _dot_nt(q_dec, st.astype(BF16))
                st_ref[bi, h] = st * jnp.exp(b_last) + _dot_tn(v, k_dec)
                r = r_ref[bi, sl, vc].astype(F32)
                o_ref[bi, sl, vc] = (_rms(o, go) * (r * jax.nn.sigmoid(r))).astype(BF16)
        return carry

    lax.fori_loop(0, T_GLA // c_sz, chunk, 0)


def _gla_rec(proj, log_a, g_o):
    bsz, seq, _ = proj.shape
    v_blk = 2 * GLA_QK // GLA_V
    return pl.pallas_call(
        _gla_rec_kernel,
        grid=(seq // T_GLA,),
        in_specs=[
            pl.BlockSpec((bsz, T_GLA, GLA_QK), lambda t: (0, t, 0)),
            pl.BlockSpec((bsz, T_GLA, GLA_QK), lambda t: (0, t, 1)),
            pl.BlockSpec((bsz, T_GLA, GLA_V), lambda t: (0, t, v_blk)),
            pl.BlockSpec((bsz, T_GLA, GLA_V), lambda t: (0, t, v_blk + 1)),
            pl.BlockSpec((bsz, T_GLA, GLA_QK), lambda t: (0, t, 0)),
            pl.BlockSpec((1, GLA_DV), lambda t: (0, 0)),
        ],
        out_specs=pl.BlockSpec((bsz, T_GLA, GLA_V), lambda t: (0, t, 0)),
        out_shape=jax.ShapeDtypeStruct((bsz, seq, GLA_V), BF16),
        scratch_shapes=[pltpu.VMEM((bsz, GLA_HEADS, GLA_DV, GLA_DK), F32)],
        compiler_params=_params("arbitrary"),
        name="gla_rec",
    )(proj, proj, proj, proj, log_a, g_o)


def _matmul_res_kernel(a_ref, w_ref, res_ref, o_ref):
    o_ref[...] = res_ref[...] + _dot(a_ref[...], w_ref[...])


def _matmul_res(a, w, res):
    m, kdim = a.shape
    n = w.shape[1]
    return pl.pallas_call(
        _matmul_res_kernel,
        grid=(m // TM_WIDE,),
        in_specs=[
            pl.BlockSpec((TM_WIDE, kdim), lambda i: (i, 0)),
            pl.BlockSpec((kdim, n), lambda i: (0, 0), pipeline_mode=pl.Buffered(1)),
            pl.BlockSpec((TM_WIDE, n), lambda i: (i, 0)),
        ],
        out_specs=pl.BlockSpec((TM_WIDE, n), lambda i: (i, 0)),
        out_shape=jax.ShapeDtypeStruct((m, n), F32),
        compiler_params=_params("parallel"),
        name="matmul_res",
    )(a, w, res)


def _q_proj_kernel(h_ref, g_ref, w_ref, gq_ref, o_ref):
    q = _dot(_rms(h_ref[...], g_ref[...]).astype(BF16), w_ref[...])
    tm = q.shape[0]
    gq = jnp.broadcast_to(gq_ref[...], (PAIR_W, tm))
    for pr in range(N_PAIRS):
        t = q[:, pr * PAIR_W:(pr + 1) * PAIR_W].T
        halves = []
        for hd in range(2):
            th = t[hd * HEAD_DIM:(hd + 1) * HEAD_DIM]
            ms = jnp.mean(th * th, axis=0, keepdims=True)
            halves.append(th * lax.rsqrt(ms + EPS))
        o_ref[pr * PAIR_W:(pr + 1) * PAIR_W, :] = (jnp.concatenate(halves, axis=0) * gq).astype(BF16)


def _q_proj(h, g, w, gq_col):
    m = h.shape[0]
    return pl.pallas_call(
        _q_proj_kernel,
        grid=(m // TM,),
        in_specs=[
            pl.BlockSpec((TM, D_MODEL), lambda i: (i, 0)),
            pl.BlockSpec((1, D_MODEL), lambda i: (0, 0)),
            pl.BlockSpec((D_MODEL, D_MODEL), lambda i: (0, 0)),
            pl.BlockSpec((PAIR_W, 1), lambda i: (0, 0)),
        ],
        out_specs=pl.BlockSpec((D_MODEL, TM), lambda i: (0, i)),
        out_shape=jax.ShapeDtypeStruct((D_MODEL, m), BF16),
        compiler_params=_params("parallel"),
        name="q_proj",
    )(h, g, w, gq_col)


def _kv_kernel(h_ref, g_ref, w_ref, gk_ref, klo_ref, khi_ref, ve_ref, vo_ref):
    kv = _dot(_rms(h_ref[...], g_ref[...]).astype(BF16), w_ref[...])
    tm = kv.shape[0]
    gk = gk_ref[...]
    zeros = jnp.zeros((tm, HEAD_DIM), F32)
    ones_t = jnp.ones((HEAD_DIM, tm), F32)
    klo, khi = [], []
    for i in range(N_KV_HEADS):
        k = _rms(kv[:, i * HEAD_DIM:(i + 1) * HEAD_DIM], gk)
        klo += [k, zeros]
        khi += [zeros, k]
    klo_ref[...] = jnp.concatenate(klo, axis=1).astype(BF16)
    khi_ref[...] = jnp.concatenate(khi, axis=1).astype(BF16)
    for i in range(N_KV_HEADS // 2):
        vt = kv[:, KV_W + i * PAIR_W:KV_W + (i + 1) * PAIR_W].T
        for hd in range(2):
            v_t = vt[hd * HEAD_DIM:(hd + 1) * HEAD_DIM]
            rows = slice((2 * i + hd) * PAIR_W, (2 * i + hd + 1) * PAIR_W)
            ve_ref[rows, :] = jnp.concatenate([v_t, ones_t], axis=0).astype(BF16)
            vo_ref[rows, :] = jnp.concatenate([ones_t, v_t], axis=0).astype(BF16)


def _shared_kv(h, g, w, gk):
    m = h.shape[0]
    wide = N_KV_HEADS * PAIR_W
    return pl.pallas_call(
        _kv_kernel,
        grid=(m // TM,),
        in_specs=[
            pl.BlockSpec((TM, D_MODEL), lambda i: (i, 0)),
            pl.BlockSpec((1, D_MODEL), lambda i: (0, 0)),
            pl.BlockSpec((D_MODEL, 2 * KV_W), lambda i: (0, 0)),
            pl.BlockSpec((1, HEAD_DIM), lambda i: (0, 0)),
        ],
        out_specs=[pl.BlockSpec((TM, wide), lambda i: (i, 0))] * 2
        + [pl.BlockSpec((wide, TM), lambda i: (0, i))] * 2,
        out_shape=[jax.ShapeDtypeStruct((m, wide), BF16)] * 2 + [jax.ShapeDtypeStruct((wide, m), BF16)] * 2,
        compiler_params=_params("parallel"),
        name="shared_kv",
    )(h, g, w, gk)


def _bias_kernel(rb_ref, sinks_ref, bucket_ref, o_ref):
    layer, later, kv, parity = (pl.program_id(a) for a in range(4))
    bucket = bucket_ref[...]
    sj = lax.broadcasted_iota(jnp.int32, bucket.shape, 0)
    qi = lax.broadcasted_iota(jnp.int32, bucket.shape, 1)
    dist = qi + BLOCK - sj
    kpos = later * BLOCK - BLOCK + sj
    mask = (dist >= 0) & (dist < WINDOW) & (kpos >= 0)
    for pair in range(GROUP_PAIRS):
        h = kv * GROUP + 2 * pair + parity
        acc = jnp.zeros(bucket.shape, F32)
        for b in range(N_BUCKETS):
            acc = jnp.where(bucket == b, rb_ref[b, h], acc)
        o_ref[:, pair * BLOCK:(pair + 1) * BLOCK] = jnp.where(
            sj == 0, sinks_ref[layer, h], jnp.where(mask, acc, -jnp.inf))


def _bias_table(rel_bias, sinks, bucket_t):
    n_layers = sinks.shape[0]
    smem = pl.BlockSpec(memory_space=pltpu.SMEM)
    return pl.pallas_call(
        _bias_kernel,
        grid=(n_layers, 2, N_KV_HEADS, 2),
        in_specs=[smem, smem, pl.BlockSpec((2 * BLOCK, BLOCK), lambda l, f, k, e: (0, 0))],
        out_specs=pl.BlockSpec((None, None, None, None, 2 * BLOCK, GROUP_PAIRS * BLOCK),
                               lambda l, f, k, e: (l, f, k, e, 0, 0)),
        out_shape=jax.ShapeDtypeStruct((n_layers, 2, N_KV_HEADS, 2, 2 * BLOCK, GROUP_PAIRS * BLOCK), F32),
        compiler_params=_params(*["parallel"] * 4),
        name="bias_table",
    )(rel_bias, sinks, bucket_t)


def _swa_kernel(q_ref, klp_ref, klc_ref, khp_ref, khc_ref, vep_ref, vec_ref, vop_ref, voc_ref,
                bias_ref, o_ref):
    k_keep = jnp.where(lax.broadcasted_iota(jnp.int32, (2 * BLOCK, PAIR_W), 0) == 0, 0.0, 1.0).astype(BF16)
    v_row = lax.broadcasted_iota(jnp.int32, (PAIR_W, 2 * BLOCK), 0)
    v_col = lax.broadcasted_iota(jnp.int32, (PAIR_W, 2 * BLOCK), 1)
    ve_keep = jnp.where((v_col == 0) & (v_row < HEAD_DIM), 0.0, 1.0).astype(BF16)
    vo_keep = jnp.where((v_col == 0) & (v_row >= HEAD_DIM), 0.0, 1.0).astype(BF16)

    for kv in range(N_KV_HEADS):
        tile = slice(kv * PAIR_W, (kv + 1) * PAIR_W)
        klo = jnp.concatenate([klp_ref[:, tile], klc_ref[:, tile]], axis=0) * k_keep
        khi = jnp.concatenate([khp_ref[:, tile], khc_ref[:, tile]], axis=0) * k_keep
        ve = jnp.concatenate([vep_ref[tile, :], vec_ref[tile, :]], axis=1) * ve_keep
        vo = jnp.concatenate([vop_ref[tile, :], voc_ref[tile, :]], axis=1) * vo_keep
        q0 = kv * GROUP_PAIRS * PAIR_W
        q_t = jnp.concatenate([q_ref[q0 + pr * PAIR_W:q0 + (pr + 1) * PAIR_W, :]
                               for pr in range(GROUP_PAIRS)], axis=1)

        def probs(k, parity):
            s = _dot(k, q_t) + bias_ref[kv, parity]
            return jnp.exp(s - jnp.max(s, axis=0, keepdims=True)).astype(BF16)

        oe = _dot(ve, probs(klo, 0))
        oo = _dot(vo, probs(khi, 1))
        num = jnp.concatenate([oe[:HEAD_DIM], oo[HEAD_DIM:]], axis=0)
        den = jnp.concatenate([oe[HEAD_DIM:], oo[:HEAD_DIM]], axis=0)
        out_t = num / den
        for pr in range(GROUP_PAIRS):
            o_ref[:, q0 + pr * PAIR_W:q0 + (pr + 1) * PAIR_W] = (
                out_t[:, pr * BLOCK:(pr + 1) * BLOCK].T.astype(BF16))


def _swa_attn(q_t, klo, khi, ve_t, vo_t, bias, layer):
    bsz, seq, wide = klo.shape
    nb = seq // BLOCK
    prev = lambda b, n: (b, jnp.maximum(n - 1, 0), 0)
    cur = lambda b, n: (b, n, 0)
    prev_t = lambda b, n: (0, b * nb + jnp.maximum(n - 1, 0))
    cur_t = lambda b, n: (0, b * nb + n)
    k_prev, k_cur = pl.BlockSpec((None, BLOCK, wide), prev), pl.BlockSpec((None, BLOCK, wide), cur)
    v_prev, v_cur = pl.BlockSpec((wide, BLOCK), prev_t), pl.BlockSpec((wide, BLOCK), cur_t)
    return pl.pallas_call(
        _swa_kernel,
        grid=(bsz, nb),
        in_specs=[
            pl.BlockSpec((D_MODEL, BLOCK), cur_t),
            k_prev, k_cur, k_prev, k_cur, v_prev, v_cur, v_prev, v_cur,
            pl.BlockSpec((None, None, N_KV_HEADS, 2, 2 * BLOCK, GROUP_PAIRS * BLOCK),
                         lambda b, n: (layer, jnp.minimum(n, 1), 0, 0, 0, 0)),
        ],
        out_specs=pl.BlockSpec((None, BLOCK, D_MODEL), cur),
        out_shape=jax.ShapeDtypeStruct((bsz, seq, D_MODEL), BF16),
        compiler_params=_params("parallel", "arbitrary"),
        name="swa_attn",
    )(q_t, klo, klo, khi, khi, ve_t, ve_t, vo_t, vo_t, bias)


def _ple_kernel(h_ref, g_ref, wgate_ref, p_ref, wple_ref, o_ref):
    x = h_ref[...]
    gate = jax.nn.sigmoid(_dot(_rms(x, g_ref[...]).astype(BF16), wgate_ref[...]))
    o_ref[...] = x + gate * _dot(p_ref[...].astype(BF16), wple_ref[...])


def _ple(h, g, w_gate_all, p_all, w_ple, layer):
    m = h.shape[0]
    return pl.pallas_call(
        _ple_kernel,
        grid=(m // TM_WIDE,),
        in_specs=[
            pl.BlockSpec((TM_WIDE, D_MODEL), lambda i: (i, 0)),
            pl.BlockSpec((1, D_MODEL), lambda i: (0, 0)),
            pl.BlockSpec((None, D_MODEL, D_MODEL), lambda i: (layer, 0, 0), pipeline_mode=pl.Buffered(1)),
            pl.BlockSpec((None, TM_WIDE, PLE_DIM), lambda i: (layer, i, 0)),
            pl.BlockSpec((PLE_DIM, D_MODEL), lambda i: (0, 0), pipeline_mode=pl.Buffered(1)),
        ],
        out_specs=pl.BlockSpec((TM_WIDE, D_MODEL), lambda i: (i, 0)),
        out_shape=jax.ShapeDtypeStruct((m, D_MODEL), F32),
        compiler_params=_params("parallel"),
        name="ple",
    )(h, g, w_gate_all, p_all, w_ple)


def _t5_bucket(d):
    max_exact = N_BUCKETS // 2
    is_small = d < max_exact
    df = jnp.maximum(d, 1).astype(F32)
    large = max_exact + (jnp.log(df / max_exact) / math.log(MAX_DISTANCE / max_exact)
                         * (N_BUCKETS - max_exact)).astype(jnp.int32)
    large = jnp.minimum(large, N_BUCKETS - 1)
    return jnp.where(is_small, d, large)


def kernel(x, p, norm_g, ffn_wg, ffn_wu, ffn_wd, ple_w, ple_gate_w, gla_w_in, gla_w_a2, gla_b_a,
           gla_norm_g, gla_w_o, kv_norm_g, w_kv, k_norm_g, rel_bias, swa_w_q, q_norm_g, sinks, swa_w_o):
    bsz, seq, d = x.shape
    depth = norm_g.shape[0]
    n_a = gla_w_in.shape[0]
    m = bsz * seq
    row = lambda v: v.reshape(1, -1)

    sj = jnp.arange(2 * BLOCK)[:, None]
    qi = jnp.arange(BLOCK)[None, :]
    bucket_t = _t5_bucket(jnp.maximum(qi + BLOCK - sj, 0)).astype(jnp.int32)
    bias = _bias_table(rel_bias, sinks.astype(F32), bucket_t)

    h = x.reshape(m, d)
    kv_tiles = None
    w_in_all = gla_w_in.astype(BF16)
    w_gate_all = ple_gate_w.astype(BF16)
    p_all = p.reshape(depth, m, PLE_DIM)
    ffn_w = (ffn_wg[0, 0].astype(BF16), ffn_wu[0, 0].astype(BF16), ffn_wd[0, 0].astype(BF16))

    def ffn(h, g, layer, sub):
        nxt = (layer, sub + 1) if sub == 0 else (layer + 1, 0)
        next_weights = (ffn_wg, ffn_wu, ffn_wd) + nxt if nxt[0] < depth else None
        return _ffn(h, g, *ffn_w, next_weights=next_weights)

    for i in range(depth):
        if i == n_a:
            klo, khi, ve_t, vo_t = _shared_kv(h, row(kv_norm_g), w_kv.astype(BF16), row(k_norm_g))
            kv_tiles = (klo.reshape(bsz, seq, -1), khi.reshape(bsz, seq, -1), ve_t, vo_t)
        h, ffn_w = ffn(h, row(norm_g[i, 0]), i, 0)
        if i < n_a:
            w_a1 = jnp.pad(gla_w_in[i][:, GLA_MAIN:], ((0, 0), (0, V7X_LANES - GLA_RANK))).astype(BF16)
            w_a2 = jnp.pad(gla_w_a2[i], ((0, V7X_LANES - GLA_RANK), (0, 0))).astype(BF16)
            proj, log_a = _gla_proj(h, row(norm_g[i, 1]), w_in_all, i, w_a1, w_a2, row(gla_b_a[i]))
            og = _gla_rec(proj.reshape(bsz, seq, GLA_MAIN), log_a.reshape(bsz, seq, GLA_QK),
                          row(gla_norm_g[i]))
            h = _matmul_res(og.reshape(m, GLA_V), gla_w_o[i].astype(BF16), h)
        else:
            j = i - n_a
            gq_col = (jnp.tile(q_norm_g[j], 2) * (HEAD_DIM ** -0.5)).reshape(PAIR_W, 1)
            q_t = _q_proj(h, row(norm_g[i, 1]), swa_w_q[j].astype(BF16), gq_col)
            o = _swa_attn(q_t, *kv_tiles, bias, j)
            h = _matmul_res(o.reshape(m, d), swa_w_o[j].astype(BF16), h)
        h, ffn_w = ffn(h, row(norm_g[i, 2]), i, 1)
        h = _ple(h, row(norm_g[i, 3]), w_gate_all, p_all, ple_w[i].astype(BF16), i)
    return h.reshape(bsz, seq, d)
```

```python
import functools
import math

import jax
import jax.numpy as jnp
from jax import lax
from jax.experimental import pallas as pl
from jax.experimental.pallas import tpu as pltpu

F32 = jnp.float32
BF16 = jnp.bfloat16

D_MODEL = 2048
D_FF = 5632
FFN_RES = 0.5
PLE_DIM = 256
EPS = 1e-6
GLA_HEADS = 4
GLA_QK = 1024
GLA_V = 2048
GLA_DK = 256
GLA_DV = 512
GLA_RANK = 16
GLA_GATE_NORM = 16.0
GLA_CHUNK = 64
GLA_MAIN = 2 * GLA_QK + 2 * GLA_V
HEAD_DIM = 64
N_Q_HEADS = 32
N_KV_HEADS = 4
GROUP = 8
KV_W = N_KV_HEADS * HEAD_DIM
WINDOW = 128
BLOCK = WINDOW
N_BUCKETS = 32
MAX_DISTANCE = WINDOW

V7X_LANES = 128
V7X_VMEM_BYTES = 64 * 1024 * 1024
VMEM_LIMIT = V7X_VMEM_BYTES - 8 * 1024 * 1024

PAIR_W = 2 * HEAD_DIM
assert PAIR_W == V7X_LANES
N_PAIRS = N_Q_HEADS // 2
GROUP_PAIRS = GROUP // 2

TM = 512
TM_WIDE = 1024
TM_FFN = 1024
TF = 512
TM_PROJ = 1024
TN_PROJ = 1536
GATE_COLS = GLA_QK // (GLA_MAIN // TN_PROJ)
T_GLA = 256


def _params(*sem):
    return pltpu.CompilerParams(dimension_semantics=sem, vmem_limit_bytes=VMEM_LIMIT)


def _rms(x, g):
    return x * lax.rsqrt(jnp.mean(x * x, axis=-1, keepdims=True) + EPS) * g


def _dot(a, b):
    return jnp.dot(a, b, preferred_element_type=F32)


def _dot_nt(a, b):
    return lax.dot_general(a, b, (((1,), (1,)), ((), ())), preferred_element_type=F32)


def _dot_tn(a, b):
    return lax.dot_general(a, b, (((0,), (0,)), ((), ())), preferred_element_type=F32)


def _ffn_kernel(h_hbm, g_ref, wg_ref, wu_ref, wd_ref, *rest, cast_next):
    if cast_next:
        nwg_ref, nwu_ref, nwd_ref, o_ref, nwg_out, nwu_out, nwd_out, x_ref, xn_ref, x_sem = rest
        nwg_out[...] = nwg_ref[...].astype(BF16)
        nwu_out[...] = nwu_ref[...].astype(BF16)
        nwd_out[...] = nwd_ref[...].astype(BF16)
    else:
        o_ref, x_ref, xn_ref, x_sem = rest
    i = pl.program_id(0)
    j = pl.program_id(1)

    def x_copy(tile):
        rows = pl.ds(pl.multiple_of(tile * TM_FFN, TM_FFN), TM_FFN)
        return pltpu.make_async_copy(h_hbm.at[rows], x_ref, x_sem)

    def swiglu_tile(xn):
        gate = _dot(xn, wg_ref[...])
        up = _dot(xn, wu_ref[...])
        act = (gate * jax.nn.sigmoid(gate) * up * FFN_RES).astype(BF16)
        return _dot(act, wd_ref[...])

    @pl.when(j == 0)
    def _():
        @pl.when(i == 0)
        def _():
            x_copy(0).start()

        x_copy(i).wait()
        x = x_ref[...]
        xn = _rms(x, g_ref[...]).astype(BF16)
        xn_ref[...] = xn
        o_ref[...] = x + swiglu_tile(xn)

    @pl.when(j > 0)
    def _():
        @pl.when((j == 1) & (i + 1 < pl.num_programs(0)))
        def _():
            x_copy(i + 1).start()

        o_ref[...] += swiglu_tile(xn_ref[...])


def _ffn(h, g, wg, wu, wd, next_weights=None):
    m = h.shape[0]
    n_i, n_j = m // TM_FFN, D_FF // TF
    in_specs = [
        pl.BlockSpec(memory_space=pl.ANY),
        pl.BlockSpec((1, D_MODEL), lambda i, j: (0, 0)),
        pl.BlockSpec((None, D_MODEL, TF), lambda i, j: (j, 0, 0)),
        pl.BlockSpec((None, D_MODEL, TF), lambda i, j: (j, 0, 0)),
        pl.BlockSpec((TF, D_MODEL), lambda i, j: (j, 0)),
    ]
    out_specs = [pl.BlockSpec((TM_FFN, D_MODEL), lambda i, j: (i, 0))]
    out_shape = [jax.ShapeDtypeStruct((m, D_MODEL), F32)]
    args = [h, g, wg, wu, wd]
    if next_weights is not None:
        nwg, nwu, nwd, layer, sub = next_weights
        rows = D_MODEL // n_i
        in_specs += [
            pl.BlockSpec((None, None, rows, TF), lambda i, j: (layer, sub, i, j)),
            pl.BlockSpec((None, None, rows, TF), lambda i, j: (layer, sub, i, j)),
            pl.BlockSpec((None, None, TF, rows), lambda i, j: (layer, sub, j, i)),
        ]
        out_specs += [
            pl.BlockSpec((None, rows, TF), lambda i, j: (j, i, 0)),
            pl.BlockSpec((None, rows, TF), lambda i, j: (j, i, 0)),
            pl.BlockSpec((TF, rows), lambda i, j: (j, i)),
        ]
        out_shape += [
            jax.ShapeDtypeStruct((n_j, D_MODEL, TF), BF16),
            jax.ShapeDtypeStruct((n_j, D_MODEL, TF), BF16),
            jax.ShapeDtypeStruct((D_FF, D_MODEL), BF16),
        ]
        args += [nwg, nwu, nwd]
    outs = pl.pallas_call(
        functools.partial(_ffn_kernel, cast_next=next_weights is not None),
        grid=(n_i, n_j),
        in_specs=in_specs,
        out_specs=out_specs,
        out_shape=out_shape,
        scratch_shapes=[
            pltpu.VMEM((TM_FFN, D_MODEL), F32),
            pltpu.VMEM((TM_FFN, D_MODEL), BF16),
            pltpu.SemaphoreType.DMA,
        ],
        compiler_params=_params("arbitrary", "arbitrary"),
        name="ffn",
    )(*args)
    return outs[0], tuple(outs[1:])


def _gla_proj_kernel(h_ref, g_ref, w_ref, wa1_ref, wa2_ref, ba_ref, proj_ref, la_ref, xn_ref, a1_ref):
    def column_step(xn, a1):
        z = _dot(a1, wa2_ref[...]) + ba_ref[...]
        log_sig = jnp.minimum(z, 0.0) - jnp.log1p(jnp.exp(-jnp.abs(z)))
        la_ref[...] = log_sig * (1.0 / GLA_GATE_NORM)
        proj_ref[...] = _dot(xn, w_ref[...]).astype(BF16)

    @pl.when(pl.program_id(1) == 0)
    def _():
        xn = _rms(h_ref[...], g_ref[...]).astype(BF16)
        a1 = _dot(xn, wa1_ref[...]).astype(BF16)
        xn_ref[...] = xn
        a1_ref[...] = a1
        column_step(xn, a1)

    @pl.when(pl.program_id(1) > 0)
    def _():
        column_step(xn_ref[...], a1_ref[...])


def _gla_proj(h, g, w_in_all, layer, w_a1, w_a2, b_a):
    m = h.shape[0]
    return pl.pallas_call(
        _gla_proj_kernel,
        grid=(m // TM_PROJ, GLA_MAIN // TN_PROJ),
        in_specs=[
            pl.BlockSpec((TM_PROJ, D_MODEL), lambda i, j: (i, 0)),
            pl.BlockSpec((1, D_MODEL), lambda i, j: (0, 0)),
            pl.BlockSpec((None, D_MODEL, TN_PROJ), lambda i, j: (layer, 0, j)),
            pl.BlockSpec((D_MODEL, V7X_LANES), lambda i, j: (0, 0)),
            pl.BlockSpec((V7X_LANES, GATE_COLS), lambda i, j: (0, j)),
            pl.BlockSpec((1, GATE_COLS), lambda i, j: (0, j)),
        ],
        out_specs=[
            pl.BlockSpec((TM_PROJ, TN_PROJ), lambda i, j: (i, j)),
            pl.BlockSpec((TM_PROJ, GATE_COLS), lambda i, j: (i, j)),
        ],
        out_shape=[
            jax.ShapeDtypeStruct((m, GLA_MAIN), BF16),
            jax.ShapeDtypeStruct((m, GLA_QK), F32),
        ],
        scratch_shapes=[pltpu.VMEM((TM_PROJ, D_MODEL), BF16), pltpu.VMEM((TM_PROJ, V7X_LANES), BF16)],
        compiler_params=_params("parallel", "arbitrary"),
        name="gla_proj",
    )(h, g, w_in_all, w_a1, w_a2, b_a)


def _gla_rec_kernel(q_ref, k_ref, v_ref, r_ref, la_ref, go_ref, o_ref, st_ref):
    c_sz = GLA_CHUNK
    bsz = q_ref.shape[0]

    @pl.when(pl.program_id(0) == 0)
    def _():
        st_ref[...] = jnp.zeros_like(st_ref)

    row = lax.broadcasted_iota(jnp.int32, (c_sz, c_sz), 0)
    col = lax.broadcasted_iota(jnp.int32, (c_sz, c_sz), 1)
    causal = col <= row
    tril = jnp.where(causal, 1.0, 0.0).astype(BF16)
    go = go_ref[...]

    def chunk(c, carry):
        sl = pl.ds(pl.multiple_of(c * c_sz, c_sz), c_sz)
        for bi in range(bsz):
            la = la_ref[bi, sl, :]
            la1 = la.astype(BF16)
            rem = la - la1.astype(F32)
            la2 = rem.astype(BF16)
            la3 = (rem - la2.astype(F32)).astype(BF16)
            b_all = _dot(tril, la1) + _dot(tril, la2) + _dot(tril, la3)
            for h in range(GLA_HEADS):
                kc = slice(h * GLA_DK, (h + 1) * GLA_DK)
                vc = slice(h * GLA_DV, (h + 1) * GLA_DV)
                b = b_all[:, kc]
                b_last = b[c_sz - 1:c_sz, :]
                q = q_ref[bi, sl, kc].astype(F32) * (GLA_DK ** -0.5)
                k = k_ref[bi, sl, kc].astype(F32)
                v = v_ref[bi, sl, vc]
                q_dec = (q * jnp.exp(b)).astype(BF16)
                k_inv = (k * jnp.exp(-b)).astype(BF16)
                k_dec = (k * jnp.exp(b_last - b)).astype(BF16)
                attn = jnp.where(causal, _dot_nt(q_dec, k_inv), 0.0)
                st = st_ref[bi, h]
                o = _dot(attn.astype(BF16), v) + _dot_nt(q_dec, st.astype(BF16))
                st_ref[bi, h] = st * jnp.exp(b_last) + _dot_tn(v, k_dec)
                r = r_ref[bi, sl, vc].astype(F32)
                o_ref[bi, sl, vc] = (_rms(o, go) * (r * jax.nn.sigmoid(r))).astype(BF16)
        return carry

    lax.fori_loop(0, T_GLA // c_sz, chunk, 0)


def _gla_rec(proj, log_a, g_o):
    bsz, seq, _ = proj.shape
    v_blk = 2 * GLA_QK // GLA_V
    return pl.pallas_call(
        _gla_rec_kernel,
        grid=(seq // T_GLA,),
        in_specs=[
            pl.BlockSpec((bsz, T_GLA, GLA_QK), lambda t: (0, t, 0)),
            pl.BlockSpec((bsz, T_GLA, GLA_QK), lambda t: (0, t, 1)),
            pl.BlockSpec((bsz, T_GLA, GLA_V), lambda t: (0, t, v_blk)),
            pl.BlockSpec((bsz, T_GLA, GLA_V), lambda t: (0, t, v_blk + 1)),
            pl.BlockSpec((bsz, T_GLA, GLA_QK), lambda t: (0, t, 0)),
            pl.BlockSpec((1, GLA_DV), lambda t: (0, 0)),
        ],
        out_specs=pl.BlockSpec((bsz, T_GLA, GLA_V), lambda t: (0, t, 0)),
        out_shape=jax.ShapeDtypeStruct((bsz, seq, GLA_V), BF16),
        scratch_shapes=[pltpu.VMEM((bsz, GLA_HEADS, GLA_DV, GLA_DK), F32)],
        compiler_params=_params("arbitrary"),
        name="gla_rec",
    )(proj, proj, proj, proj, log_a, g_o)


def _matmul_res_kernel(a_ref, w_ref, res_ref, o_ref):
    o_ref[...] = res_ref[...] + _dot(a_ref[...], w_ref[...])


def _matmul_res(a, w, res):
    m, kdim = a.shape
    n = w.shape[1]
    return pl.pallas_call(
        _matmul_res_kernel,
        grid=(m // TM_WIDE,),
        in_specs=[
            pl.BlockSpec((TM_WIDE, kdim), lambda i: (i, 0)),
            pl.BlockSpec((kdim, n), lambda i: (0, 0), pipeline_mode=pl.Buffered(1)),
            pl.BlockSpec((TM_WIDE, n), lambda i: (i, 0)),
        ],
        out_specs=pl.BlockSpec((TM_WIDE, n), lambda i: (i, 0)),
        out_shape=jax.ShapeDtypeStruct((m, n), F32),
        compiler_params=_params("parallel"),
        name="matmul_res",
    )(a, w, res)


def _q_proj_kernel(h_ref, g_ref, w_ref, gq_ref, o_ref):
    q = _dot(_rms(h_ref[...], g_ref[...]).astype(BF16), w_ref[...])
    tm = q.shape[0]
    gq = jnp.broadcast_to(gq_ref[...], (PAIR_W, tm))
    for pr in range(N_PAIRS):
        t = q[:, pr * PAIR_W:(pr + 1) * PAIR_W].T
        halves = []
        for hd in range(2):
            th = t[hd * HEAD_DIM:(hd + 1) * HEAD_DIM]
            ms = jnp.mean(th * th, axis=0, keepdims=True)
            halves.append(th * lax.rsqrt(ms + EPS))
        qn = (jnp.concatenate(halves, axis=0) * gq).astype(BF16)
        for tb in range(tm // BLOCK):
            o_ref[tb, pr * PAIR_W:(pr + 1) * PAIR_W, :] = qn[:, tb * BLOCK:(tb + 1) * BLOCK]


def _q_proj(h, g, w, gq_col):
    m = h.shape[0]
    return pl.pallas_call(
        _q_proj_kernel,
        grid=(m // TM,),
        in_specs=[
            pl.BlockSpec((TM, D_MODEL), lambda i: (i, 0)),
            pl.BlockSpec((1, D_MODEL), lambda i: (0, 0)),
            pl.BlockSpec((D_MODEL, D_MODEL), lambda i: (0, 0)),
            pl.BlockSpec((PAIR_W, 1), lambda i: (0, 0)),
        ],
        out_specs=pl.BlockSpec((TM // BLOCK, D_MODEL, BLOCK), lambda i: (i, 0, 0)),
        out_shape=jax.ShapeDtypeStruct((m // BLOCK, D_MODEL, BLOCK), BF16),
        compiler_params=_params("parallel"),
        name="q_proj",
    )(h, g, w, gq_col)


def _kv_kernel(h_ref, g_ref, w_ref, gk_ref, klo_ref, khi_ref, ve_ref, vo_ref):
    kv = _dot(_rms(h_ref[...], g_ref[...]).astype(BF16), w_ref[...])
    tm = kv.shape[0]
    gk = gk_ref[...]
    zeros = jnp.zeros((tm, HEAD_DIM), F32)
    ones_t = jnp.ones((HEAD_DIM, tm), F32)
    klo, khi = [], []
    for i in range(N_KV_HEADS):
        k = _rms(kv[:, i * HEAD_DIM:(i + 1) * HEAD_DIM], gk)
        klo += [k, zeros]
        khi += [zeros, k]
    klo_ref[...] = jnp.concatenate(klo, axis=1).astype(BF16)
    khi_ref[...] = jnp.concatenate(khi, axis=1).astype(BF16)
    for i in range(N_KV_HEADS // 2):
        vt = kv[:, KV_W + i * PAIR_W:KV_W + (i + 1) * PAIR_W].T
        for hd in range(2):
            v_t = vt[hd * HEAD_DIM:(hd + 1) * HEAD_DIM]
            rows = slice((2 * i + hd) * PAIR_W, (2 * i + hd + 1) * PAIR_W)
            ve = jnp.concatenate([v_t, ones_t], axis=0).astype(BF16)
            vo = jnp.concatenate([ones_t, v_t], axis=0).astype(BF16)
            for tb in range(tm // BLOCK):
                ve_ref[tb, rows, :] = ve[:, tb * BLOCK:(tb + 1) * BLOCK]
                vo_ref[tb, rows, :] = vo[:, tb * BLOCK:(tb + 1) * BLOCK]


def _shared_kv(h, g, w, gk):
    m = h.shape[0]
    wide = N_KV_HEADS * PAIR_W
    return pl.pallas_call(
        _kv_kernel,
        grid=(m // TM,),
        in_specs=[
            pl.BlockSpec((TM, D_MODEL), lambda i: (i, 0)),
            pl.BlockSpec((1, D_MODEL), lambda i: (0, 0)),
            pl.BlockSpec((D_MODEL, 2 * KV_W), lambda i: (0, 0)),
            pl.BlockSpec((1, HEAD_DIM), lambda i: (0, 0)),
        ],
        out_specs=[pl.BlockSpec((TM, wide), lambda i: (i, 0))] * 2
        + [pl.BlockSpec((TM // BLOCK, wide, BLOCK), lambda i: (i, 0, 0))] * 2,
        out_shape=[jax.ShapeDtypeStruct((m, wide), BF16)] * 2
        + [jax.ShapeDtypeStruct((m // BLOCK, wide, BLOCK), BF16)] * 2,
        compiler_params=_params("parallel"),
        name="shared_kv",
    )(h, g, w, gk)


def _bias_kernel(rb_ref, sinks_ref, bucket_ref, o_ref):
    layer, later, kv, parity = (pl.program_id(a) for a in range(4))
    bucket = bucket_ref[...]
    sj = lax.broadcasted_iota(jnp.int32, bucket.shape, 0)
    qi = lax.broadcasted_iota(jnp.int32, bucket.shape, 1)
    dist = qi + BLOCK - sj
    kpos = later * BLOCK - BLOCK + sj
    mask = (dist >= 0) & (dist < WINDOW) & (kpos >= 0)
    for pair in range(GROUP_PAIRS):
        h = kv * GROUP + 2 * pair + parity
        acc = jnp.zeros(bucket.shape, F32)
        for b in range(N_BUCKETS):
            acc = jnp.where(bucket == b, rb_ref[b, h], acc)
        o_ref[:, pair * BLOCK:(pair + 1) * BLOCK] = jnp.where(
            sj == 0, sinks_ref[layer, h], jnp.where(mask, acc, -jnp.inf))


def _bias_table(rel_bias, sinks, bucket_t):
    n_layers = sinks.shape[0]
    smem = pl.BlockSpec(memory_space=pltpu.SMEM)
    return pl.pallas_call(
        _bias_kernel,
        grid=(n_layers, 2, N_KV_HEADS, 2),
        in_specs=[smem, smem, pl.BlockSpec((2 * BLOCK, BLOCK), lambda l, f, k, e: (0, 0))],
        out_specs=pl.BlockSpec((None, None, None, None, 2 * BLOCK, GROUP_PAIRS * BLOCK),
                               lambda l, f, k, e: (l, f, k, e, 0, 0)),
        out_shape=jax.ShapeDtypeStruct((n_layers, 2, N_KV_HEADS, 2, 2 * BLOCK, GROUP_PAIRS * BLOCK), F32),
        compiler_params=_params(*["parallel"] * 4),
        name="bias_table",
    )(rel_bias, sinks, bucket_t)


def _swa_kernel(q_ref, klp_ref, klc_ref, khp_ref, khc_ref, vep_ref, vec_ref, vop_ref, voc_ref,
                bias_ref, o_ref):
    k_keep = jnp.where(lax.broadcasted_iota(jnp.int32, (2 * BLOCK, PAIR_W), 0) == 0, 0.0, 1.0).astype(BF16)
    v_row = lax.broadcasted_iota(jnp.int32, (PAIR_W, 2 * BLOCK), 0)
    v_col = lax.broadcasted_iota(jnp.int32, (PAIR_W, 2 * BLOCK), 1)
    ve_keep = jnp.where((v_col == 0) & (v_row < HEAD_DIM), 0.0, 1.0).astype(BF16)
    vo_keep = jnp.where((v_col == 0) & (v_row >= HEAD_DIM), 0.0, 1.0).astype(BF16)

    for kv in range(N_KV_HEADS):
        tile = slice(kv * PAIR_W, (kv + 1) * PAIR_W)
        klo = jnp.concatenate([klp_ref[:, tile], klc_ref[:, tile]], axis=0) * k_keep
        khi = jnp.concatenate([khp_ref[:, tile], khc_ref[:, tile]], axis=0) * k_keep
        ve = jnp.concatenate([vep_ref[tile, :], vec_ref[tile, :]], axis=1) * ve_keep
        vo = jnp.concatenate([vop_ref[tile, :], voc_ref[tile, :]], axis=1) * vo_keep
        q0 = kv * GROUP_PAIRS * PAIR_W
        q_t = jnp.concatenate([q_ref[q0 + pr * PAIR_W:q0 + (pr + 1) * PAIR_W, :]
                               for pr in range(GROUP_PAIRS)], axis=1)

        def probs(k, parity):
            s = _dot(k, q_t) + bias_ref[kv, parity]
            return jnp.exp(s - jnp.max(s, axis=0, keepdims=True)).astype(BF16)

        oe = _dot(ve, probs(klo, 0))
        oo = _dot(vo, probs(khi, 1))
        num = jnp.concatenate([oe[:HEAD_DIM], oo[HEAD_DIM:]], axis=0)
        den = jnp.concatenate([oe[HEAD_DIM:], oo[:HEAD_DIM]], axis=0)
        out_t = num / den
        for pr in range(GROUP_PAIRS):
            o_ref[:, q0 + pr * PAIR_W:q0 + (pr + 1) * PAIR_W] = (
                out_t[:, pr * BLOCK:(pr + 1) * BLOCK].T.astype(BF16))


def _swa_attn(q_t, klo, khi, ve_t, vo_t, bias, layer):
    bsz, seq, wide = klo.shape
    nb = seq // BLOCK
    prev = lambda b, n: (b, jnp.maximum(n - 1, 0), 0)
    cur = lambda b, n: (b, n, 0)
    prev_t = lambda b, n: (b * nb + jnp.maximum(n - 1, 0), 0, 0)
    cur_t = lambda b, n: (b * nb + n, 0, 0)
    k_prev, k_cur = pl.BlockSpec((None, BLOCK, wide), prev), pl.BlockSpec((None, BLOCK, wide), cur)
    v_prev, v_cur = pl.BlockSpec((None, wide, BLOCK), prev_t), pl.BlockSpec((None, wide, BLOCK), cur_t)
    return pl.pallas_call(
        _swa_kernel,
        grid=(bsz, nb),
        in_specs=[
            pl.BlockSpec((None, D_MODEL, BLOCK), cur_t),
            k_prev, k_cur, k_prev, k_cur, v_prev, v_cur, v_prev, v_cur,
            pl.BlockSpec((None, None, N_KV_HEADS, 2, 2 * BLOCK, GROUP_PAIRS * BLOCK),
                         lambda b, n: (layer, jnp.minimum(n, 1), 0, 0, 0, 0)),
        ],
        out_specs=pl.BlockSpec((None, BLOCK, D_MODEL), cur),
        out_shape=jax.ShapeDtypeStruct((bsz, seq, D_MODEL), BF16),
        compiler_params=_params("parallel", "arbitrary"),
        name="swa_attn",
    )(q_t, klo, klo, khi, khi, ve_t, ve_t, vo_t, vo_t, bias)


def _ple_kernel(h_ref, g_ref, wgate_ref, p_ref, wple_ref, o_ref):
    x = h_ref[...]
    gate = jax.nn.sigmoid(_dot(_rms(x, g_ref[...]).astype(BF16), wgate_ref[...]))
    o_ref[...] = x + gate * _dot(p_ref[...].astype(BF16), wple_ref[...])


def _ple(h, g, w_gate_all, p_all, w_ple, layer):
    m = h.shape[0]
    return pl.pallas_call(
        _ple_kernel,
        grid=(m // TM_WIDE,),
        in_specs=[
            pl.BlockSpec((TM_WIDE, D_MODEL), lambda i: (i, 0)),
            pl.BlockSpec((1, D_MODEL), lambda i: (0, 0)),
            pl.BlockSpec((None, D_MODEL, D_MODEL), lambda i: (layer, 0, 0), pipeline_mode=pl.Buffered(1)),
            pl.BlockSpec((None, TM_WIDE, PLE_DIM), lambda i: (layer, i, 0)),
            pl.BlockSpec((PLE_DIM, D_MODEL), lambda i: (0, 0), pipeline_mode=pl.Buffered(1)),
        ],
        out_specs=pl.BlockSpec((TM_WIDE, D_MODEL), lambda i: (i, 0)),
        out_shape=jax.ShapeDtypeStruct((m, D_MODEL), F32),
        compiler_params=_params("parallel"),
        name="ple",
    )(h, g, w_gate_all, p_all, w_ple)


def _t5_bucket(d):
    max_exact = N_BUCKETS // 2
    is_small = d < max_exact
    df = jnp.maximum(d, 1).astype(F32)
    large = max_exact + (jnp.log(df / max_exact) / math.log(MAX_DISTANCE / max_exact)
                         * (N_BUCKETS - max_exact)).astype(jnp.int32)
    large = jnp.minimum(large, N_BUCKETS - 1)
    return jnp.where(is_small, d, large)


def kernel(x, p, norm_g, ffn_wg, ffn_wu, ffn_wd, ple_w, ple_gate_w, gla_w_in, gla_w_a2, gla_b_a,
           gla_norm_g, gla_w_o, kv_norm_g, w_kv, k_norm_g, rel_bias, swa_w_q, q_norm_g, sinks, swa_w_o):
    bsz, seq, d = x.shape
    depth = norm_g.shape[0]
    n_a = gla_w_in.shape[0]
    m = bsz * seq
    row = lambda v: v.reshape(1, -1)

    sj = jnp.arange(2 * BLOCK)[:, None]
    qi = jnp.arange(BLOCK)[None, :]
    bucket_t = _t5_bucket(jnp.maximum(qi + BLOCK - sj, 0)).astype(jnp.int32)
    bias = _bias_table(rel_bias, sinks.astype(F32), bucket_t)

    h = x.reshape(m, d)
    kv_tiles = None
    w_in_all = gla_w_in.astype(BF16)
    w_gate_all = ple_gate_w.astype(BF16)
    p_all = p.reshape(depth, m, PLE_DIM)
    col_tiles = lambda w: w.astype(BF16).reshape(d, D_FF // TF, TF).transpose(1, 0, 2)
    ffn_w = (col_tiles(ffn_wg[0, 0]), col_tiles(ffn_wu[0, 0]), ffn_wd[0, 0].astype(BF16))

    def ffn(h, g, layer, sub):
        nxt = (layer, sub + 1) if sub == 0 else (layer + 1, 0)
        next_weights = (ffn_wg, ffn_wu, ffn_wd) + nxt if nxt[0] < depth else None
        return _ffn(h, g, *ffn_w, next_weights=next_weights)

    for i in range(depth):
        if i == n_a:
            klo, khi, ve_t, vo_t = _shared_kv(h, row(kv_norm_g), w_kv.astype(BF16), row(k_norm_g))
            kv_tiles = (klo.reshape(bsz, seq, -1), khi.reshape(bsz, seq, -1), ve_t, vo_t)
        h, ffn_w = ffn(h, row(norm_g[i, 0]), i, 0)
        if i < n_a:
            w_a1 = jnp.pad(gla_w_in[i][:, GLA_MAIN:], ((0, 0), (0, V7X_LANES - GLA_RANK))).astype(BF16)
            w_a2 = jnp.pad(gla_w_a2[i], ((0, V7X_LANES - GLA_RANK), (0, 0))).astype(BF16)
            proj, log_a = _gla_proj(h, row(norm_g[i, 1]), w_in_all, i, w_a1, w_a2, row(gla_b_a[i]))
            og = _gla_rec(proj.reshape(bsz, seq, GLA_MAIN), log_a.reshape(bsz, seq, GLA_QK),
                          row(gla_norm_g[i]))
            h = _matmul_res(og.reshape(m, GLA_V), gla_w_o[i].astype(BF16), h)
        else:
            j = i - n_a
            gq_col = (jnp.tile(q_norm_g[j], 2) * (HEAD_DIM ** -0.5)).reshape(PAIR_W, 1)
            q_t = _q_proj(h, row(norm_g[i, 1]), swa_w_q[j].astype(BF16), gq_col)
            o = _swa_attn(q_t, *kv_tiles, bias, j)
            h = _matmul_res(o.reshape(m, d), swa_w_o[j].astype(BF16), h)
        h, ffn_w = ffn(h, row(norm_g[i, 2]), i, 1)
        h = _ple(h, row(norm_g[i, 3]), w_gate_all, p_all, ple_w[i].astype(BF16), i)
    return h.reshape(bsz, seq, d)
```

```python
import functools
import math

import jax
import jax.numpy as jnp
from jax import lax
from jax.experimental import pallas as pl
from jax.experimental.pallas import tpu as pltpu

F32 = jnp.float32
BF16 = jnp.bfloat16

D_MODEL = 2048
D_FF = 5632
FFN_RES = 0.5
PLE_DIM = 256
EPS = 1e-6
GLA_HEADS = 4
GLA_QK = 1024
GLA_V = 2048
GLA_DK = 256
GLA_DV = 512
GLA_RANK = 16
GLA_GATE_NORM = 16.0
GLA_CHUNK = 64
GLA_MAIN = 2 * GLA_QK + 2 * GLA_V
HEAD_DIM = 64
N_Q_HEADS = 32
N_KV_HEADS = 4
GROUP = 8
KV_W = N_KV_HEADS * HEAD_DIM
WINDOW = 128
BLOCK = WINDOW
N_BUCKETS = 32
MAX_DISTANCE = WINDOW

V7X_LANES = 128
V7X_VMEM_BYTES = 64 * 1024 * 1024
VMEM_LIMIT = V7X_VMEM_BYTES - 8 * 1024 * 1024

PAIR_W = 2 * HEAD_DIM
assert PAIR_W == V7X_LANES
N_PAIRS = N_Q_HEADS // 2
GROUP_PAIRS = GROUP // 2

TM = 512
TM_WIDE = 1024
TM_FFN = 1024
TF = 512
TM_PROJ = 1024
TN_PROJ = 1536
GATE_COLS = GLA_QK // (GLA_MAIN // TN_PROJ)
T_GLA = 256


def _params(*sem):
    return pltpu.CompilerParams(dimension_semantics=sem, vmem_limit_bytes=VMEM_LIMIT)


def _rms(x, g):
    return x * lax.rsqrt(jnp.mean(x * x, axis=-1, keepdims=True) + EPS) * g


def _dot(a, b):
    return jnp.dot(a, b, preferred_element_type=F32)


def _dot_nt(a, b):
    return lax.dot_general(a, b, (((1,), (1,)), ((), ())), preferred_element_type=F32)


def _dot_tn(a, b):
    return lax.dot_general(a, b, (((0,), (0,)), ((), ())), preferred_element_type=F32)


def _ffn_kernel(h_hbm, g_ref, wg_ref, wu_ref, wd_ref, *rest, n_casts):
    cast_in, (o_ref, *cast_out), (x_ref, xn_ref, x_sem) = (
        rest[:n_casts], rest[n_casts:2 * n_casts + 1], rest[2 * n_casts + 1:])
    for src, dst in zip(cast_in, cast_out):
        dst[...] = src[...].astype(BF16)
    i = pl.program_id(0)
    j = pl.program_id(1)

    def x_copy(tile):
        rows = pl.ds(pl.multiple_of(tile * TM_FFN, TM_FFN), TM_FFN)
        return pltpu.make_async_copy(h_hbm.at[rows], x_ref, x_sem)

    def swiglu_tile(xn):
        gate = _dot(xn, wg_ref[...])
        up = _dot(xn, wu_ref[...])
        act = (gate * jax.nn.sigmoid(gate) * up * FFN_RES).astype(BF16)
        return _dot(act, wd_ref[...])

    @pl.when(j == 0)
    def _():
        @pl.when(i == 0)
        def _():
            x_copy(0).start()

        x_copy(i).wait()
        x = x_ref[...]
        xn = _rms(x, g_ref[...]).astype(BF16)
        xn_ref[...] = xn
        o_ref[...] = x + swiglu_tile(xn)

    @pl.when(j > 0)
    def _():
        @pl.when((j == 1) & (i + 1 < pl.num_programs(0)))
        def _():
            x_copy(i + 1).start()

        o_ref[...] += swiglu_tile(xn_ref[...])


def _stacked_cast_specs(w, n_i, n_j):
    layers, rows, cols = w.shape
    per_layer = 1 << ((n_j // layers).bit_length() - 1)
    blk = rows // (n_i * per_layer)
    assert blk * n_i * per_layer == rows
    used = layers * per_layer

    def index(i, j):
        jj = jnp.minimum(j, used - 1)
        return (jj // per_layer, i * per_layer + jj % per_layer, 0)

    spec = pl.BlockSpec((None, blk, cols), index)
    return w, spec, spec, jax.ShapeDtypeStruct(w.shape, BF16)


def _ffn(h, g, wg, wu, wd, next_weights=None, extra_casts=()):
    m = h.shape[0]
    n_i, n_j = m // TM_FFN, D_FF // TF
    in_specs = [
        pl.BlockSpec(memory_space=pl.ANY),
        pl.BlockSpec((1, D_MODEL), lambda i, j: (0, 0)),
        pl.BlockSpec((D_MODEL, TF), lambda i, j: (0, j)),
        pl.BlockSpec((D_MODEL, TF), lambda i, j: (0, j)),
        pl.BlockSpec((TF, D_MODEL), lambda i, j: (j, 0)),
    ]
    out_specs = [pl.BlockSpec((TM_FFN, D_MODEL), lambda i, j: (i, 0))]
    out_shape = [jax.ShapeDtypeStruct((m, D_MODEL), F32)]
    casts = []
    if next_weights is not None:
        nwg, nwu, nwd, layer, sub = next_weights
        rows = D_MODEL // n_i
        up_in = pl.BlockSpec((None, None, rows, TF), lambda i, j: (layer, sub, i, j))
        up_out = pl.BlockSpec((rows, TF), lambda i, j: (i, j))
        up_shape = jax.ShapeDtypeStruct((D_MODEL, D_FF), BF16)
        casts += [
            (nwg, up_in, up_out, up_shape),
            (nwu, up_in, up_out, up_shape),
            (nwd, pl.BlockSpec((None, None, TF, rows), lambda i, j: (layer, sub, j, i)),
             pl.BlockSpec((TF, rows), lambda i, j: (j, i)), jax.ShapeDtypeStruct((D_FF, D_MODEL), BF16)),
        ]
    casts += [_stacked_cast_specs(w, n_i, n_j) for w in extra_casts]
    outs = pl.pallas_call(
        functools.partial(_ffn_kernel, n_casts=len(casts)),
        grid=(n_i, n_j),
        in_specs=in_specs + [c[1] for c in casts],
        out_specs=out_specs + [c[2] for c in casts],
        out_shape=out_shape + [c[3] for c in casts],
        scratch_shapes=[
            pltpu.VMEM((TM_FFN, D_MODEL), F32),
            pltpu.VMEM((TM_FFN, D_MODEL), BF16),
            pltpu.SemaphoreType.DMA,
        ],
        compiler_params=_params("arbitrary", "arbitrary"),
        name="ffn",
    )(h, g, wg, wu, wd, *[c[0] for c in casts])
    n_next = 0 if next_weights is None else 3
    return outs[0], tuple(outs[1:1 + n_next]), tuple(outs[1 + n_next:])


def _gla_proj_kernel(h_ref, g_ref, w_ref, wa1_ref, wa2_ref, ba_ref, proj_ref, la_ref, xn_ref, a1_ref):
    def column_step(xn, a1):
        z = _dot(a1, wa2_ref[...]) + ba_ref[...]
        log_sig = jnp.minimum(z, 0.0) - jnp.log1p(jnp.exp(-jnp.abs(z)))
        la_ref[...] = log_sig * (1.0 / GLA_GATE_NORM)
        proj_ref[...] = _dot(xn, w_ref[...]).astype(BF16)

    @pl.when(pl.program_id(1) == 0)
    def _():
        xn = _rms(h_ref[...], g_ref[...]).astype(BF16)
        a1 = _dot(xn, wa1_ref[...]).astype(BF16)
        xn_ref[...] = xn
        a1_ref[...] = a1
        column_step(xn, a1)

    @pl.when(pl.program_id(1) > 0)
    def _():
        column_step(xn_ref[...], a1_ref[...])


def _gla_proj(h, g, w_in_all, layer, w_a1, w_a2, b_a):
    m = h.shape[0]
    return pl.pallas_call(
        _gla_proj_kernel,
        grid=(m // TM_PROJ, GLA_MAIN // TN_PROJ),
        in_specs=[
            pl.BlockSpec((TM_PROJ, D_MODEL), lambda i, j: (i, 0)),
            pl.BlockSpec((1, D_MODEL), lambda i, j: (0, 0)),
            pl.BlockSpec((None, D_MODEL, TN_PROJ), lambda i, j: (layer, 0, j)),
            pl.BlockSpec((D_MODEL, V7X_LANES), lambda i, j: (0, 0)),
            pl.BlockSpec((V7X_LANES, GATE_COLS), lambda i, j: (0, j)),
            pl.BlockSpec((1, GATE_COLS), lambda i, j: (0, j)),
        ],
        out_specs=[
            pl.BlockSpec((TM_PROJ, TN_PROJ), lambda i, j: (i, j)),
            pl.BlockSpec((TM_PROJ, GATE_COLS), lambda i, j: (i, j)),
        ],
        out_shape=[
            jax.ShapeDtypeStruct((m, GLA_MAIN), BF16),
            jax.ShapeDtypeStruct((m, GLA_QK), F32),
        ],
        scratch_shapes=[pltpu.VMEM((TM_PROJ, D_MODEL), BF16), pltpu.VMEM((TM_PROJ, V7X_LANES), BF16)],
        compiler_params=_params("parallel", "arbitrary"),
        name="gla_proj",
    )(h, g, w_in_all, w_a1, w_a2, b_a)


def _gla_rec_kernel(q_ref, k_ref, v_ref, r_ref, la_ref, go_ref, o_ref, st_ref):
    c_sz = GLA_CHUNK
    bsz = q_ref.shape[0]

    @pl.when(pl.program_id(0) == 0)
    def _():
        st_ref[...] = jnp.zeros_like(st_ref)

    row = lax.broadcasted_iota(jnp.int32, (c_sz, c_sz), 0)
    col = lax.broadcasted_iota(jnp.int32, (c_sz, c_sz), 1)
    causal = col <= row
    tril = jnp.where(causal, 1.0, 0.0).astype(BF16)
    go = go_ref[...]

    def chunk(c, carry):
        sl = pl.ds(pl.multiple_of(c * c_sz, c_sz), c_sz)
        for bi in range(bsz):
            la = la_ref[bi, sl, :]
            la1 = la.astype(BF16)
            rem = la - la1.astype(F32)
            la2 = rem.astype(BF16)
            la3 = (rem - la2.astype(F32)).astype(BF16)
            b_all = _dot(tril, la1) + _dot(tril, la2) + _dot(tril, la3)
            for h in range(GLA_HEADS):
                kc = slice(h * GLA_DK, (h + 1) * GLA_DK)
                vc = slice(h * GLA_DV, (h + 1) * GLA_DV)
                b = b_all[:, kc]
                b_last = b[c_sz - 1:c_sz, :]
                q = q_ref[bi, sl, kc].astype(F32) * (GLA_DK ** -0.5)
                k = k_ref[bi, sl, kc].astype(F32)
                v = v_ref[bi, sl, vc]
                q_dec = (q * jnp.exp(b)).astype(BF16)
                k_inv = (k * jnp.exp(-b)).astype(BF16)
                k_dec = (k * jnp.exp(b_last - b)).astype(BF16)
                attn = jnp.where(causal, _dot_nt(q_dec, k_inv), 0.0)
                st = st_ref[bi, h]
                o = _dot(attn.astype(BF16), v) + _dot_nt(q_dec, st.astype(BF16))
                st_ref[bi, h] = st * jnp.exp(b_last) + _dot_tn(v, k_dec)
                r = r_ref[bi, sl, vc].astype(F32)
                o_ref[bi, sl, vc] = (_rms(o, go) * (r * jax.nn.sigmoid(r))).astype(BF16)
        return carry

    lax.fori_loop(0, T_GLA // c_sz, chunk, 0)


def _gla_rec(proj, log_a, g_o):
    bsz, seq, _ = proj.shape
    v_blk = 2 * GLA_QK // GLA_V
    return pl.pallas_call(
        _gla_rec_kernel,
        grid=(seq // T_GLA,),
        in_specs=[
            pl.BlockSpec((bsz, T_GLA, GLA_QK), lambda t: (0, t, 0)),
            pl.BlockSpec((bsz, T_GLA, GLA_QK), lambda t: (0, t, 1)),
            pl.BlockSpec((bsz, T_GLA, GLA_V), lambda t: (0, t, v_blk)),
            pl.BlockSpec((bsz, T_GLA, GLA_V), lambda t: (0, t, v_blk + 1)),
            pl.BlockSpec((bsz, T_GLA, GLA_QK), lambda t: (0, t, 0)),
            pl.BlockSpec((1, GLA_DV), lambda t: (0, 0)),
        ],
        out_specs=pl.BlockSpec((bsz, T_GLA, GLA_V), lambda t: (0, t, 0)),
        out_shape=jax.ShapeDtypeStruct((bsz, seq, GLA_V), BF16),
        scratch_shapes=[pltpu.VMEM((bsz, GLA_HEADS, GLA_DV, GLA_DK), F32)],
        compiler_params=_params("arbitrary"),
        name="gla_rec",
    )(proj, proj, proj, proj, log_a, g_o)


def _matmul_res_kernel(a_ref, w_ref, res_ref, o_ref):
    o_ref[...] = res_ref[...] + _dot(a_ref[...], w_ref[...])


def _matmul_res(a, w, res):
    m, kdim = a.shape
    n = w.shape[1]
    return pl.pallas_call(
        _matmul_res_kernel,
        grid=(m // TM_WIDE,),
        in_specs=[
            pl.BlockSpec((TM_WIDE, kdim), lambda i: (i, 0)),
            pl.BlockSpec((kdim, n), lambda i: (0, 0), pipeline_mode=pl.Buffered(1)),
            pl.BlockSpec((TM_WIDE, n), lambda i: (i, 0)),
        ],
        out_specs=pl.BlockSpec((TM_WIDE, n), lambda i: (i, 0)),
        out_shape=jax.ShapeDtypeStruct((m, n), F32),
        compiler_params=_params("parallel"),
        name="matmul_res",
    )(a, w, res)


def _q_proj_kernel(h_ref, g_ref, w_ref, gq_ref, o_ref):
    q = _dot(_rms(h_ref[...], g_ref[...]).astype(BF16), w_ref[...])
    tm = q.shape[0]
    gq = jnp.broadcast_to(gq_ref[...], (PAIR_W, tm))
    for pr in range(N_PAIRS):
        t = q[:, pr * PAIR_W:(pr + 1) * PAIR_W].T
        halves = []
        for hd in range(2):
            th = t[hd * HEAD_DIM:(hd + 1) * HEAD_DIM]
            ms = jnp.mean(th * th, axis=0, keepdims=True)
            halves.append(th * lax.rsqrt(ms + EPS))
        o_ref[pr * PAIR_W:(pr + 1) * PAIR_W, :] = (jnp.concatenate(halves, axis=0) * gq).astype(BF16)


def _q_proj(h, g, w, gq_col):
    m = h.shape[0]
    return pl.pallas_call(
        _q_proj_kernel,
        grid=(m // TM,),
        in_specs=[
            pl.BlockSpec((TM, D_MODEL), lambda i: (i, 0)),
            pl.BlockSpec((1, D_MODEL), lambda i: (0, 0)),
            pl.BlockSpec((D_MODEL, D_MODEL), lambda i: (0, 0)),
            pl.BlockSpec((PAIR_W, 1), lambda i: (0, 0)),
        ],
        out_specs=pl.BlockSpec((D_MODEL, TM), lambda i: (0, i)),
        out_shape=jax.ShapeDtypeStruct((D_MODEL, m), BF16),
        compiler_params=_params("parallel"),
        name="q_proj",
    )(h, g, w, gq_col)


def _kv_kernel(h_ref, g_ref, w_ref, gk_ref, klo_ref, khi_ref, ve_ref, vo_ref):
    kv = _dot(_rms(h_ref[...], g_ref[...]).astype(BF16), w_ref[...])
    tm = kv.shape[0]
    gk = gk_ref[...]
    zeros = jnp.zeros((tm, HEAD_DIM), F32)
    ones_t = jnp.ones((HEAD_DIM, tm), F32)
    klo, khi = [], []
    for i in range(N_KV_HEADS):
        k = _rms(kv[:, i * HEAD_DIM:(i + 1) * HEAD_DIM], gk)
        klo += [k, zeros]
        khi += [zeros, k]
    klo_ref[...] = jnp.concatenate(klo, axis=1).astype(BF16)
    khi_ref[...] = jnp.concatenate(khi, axis=1).astype(BF16)
    for i in range(N_KV_HEADS // 2):
        vt = kv[:, KV_W + i * PAIR_W:KV_W + (i + 1) * PAIR_W].T
        for hd in range(2):
            v_t = vt[hd * HEAD_DIM:(hd + 1) * HEAD_DIM]
            rows = slice((2 * i + hd) * PAIR_W, (2 * i + hd + 1) * PAIR_W)
            ve_ref[rows, :] = jnp.concatenate([v_t, ones_t], axis=0).astype(BF16)
            vo_ref[rows, :] = jnp.concatenate([ones_t, v_t], axis=0).astype(BF16)


def _shared_kv(h, g, w, gk):
    m = h.shape[0]
    wide = N_KV_HEADS * PAIR_W
    return pl.pallas_call(
        _kv_kernel,
        grid=(m // TM,),
        in_specs=[
            pl.BlockSpec((TM, D_MODEL), lambda i: (i, 0)),
            pl.BlockSpec((1, D_MODEL), lambda i: (0, 0)),
            pl.BlockSpec((D_MODEL, 2 * KV_W), lambda i: (0, 0)),
            pl.BlockSpec((1, HEAD_DIM), lambda i: (0, 0)),
        ],
        out_specs=[pl.BlockSpec((TM, wide), lambda i: (i, 0))] * 2
        + [pl.BlockSpec((wide, TM), lambda i: (0, i))] * 2,
        out_shape=[jax.ShapeDtypeStruct((m, wide), BF16)] * 2 + [jax.ShapeDtypeStruct((wide, m), BF16)] * 2,
        compiler_params=_params("parallel"),
        name="shared_kv",
    )(h, g, w, gk)


def _bias_kernel(rb_ref, sinks_ref, bucket_ref, o_ref):
    layer, later, kv, parity = (pl.program_id(a) for a in range(4))
    bucket = bucket_ref[...]
    sj = lax.broadcasted_iota(jnp.int32, bucket.shape, 0)
    qi = lax.broadcasted_iota(jnp.int32, bucket.shape, 1)
    dist = qi + BLOCK - sj
    kpos = later * BLOCK - BLOCK + sj
    mask = (dist >= 0) & (dist < WINDOW) & (kpos >= 0)
    for pair in range(GROUP_PAIRS):
        h = kv * GROUP + 2 * pair + parity
        acc = jnp.zeros(bucket.shape, F32)
        for b in range(N_BUCKETS):
            acc = jnp.where(bucket == b, rb_ref[b, h], acc)
        o_ref[:, pair * BLOCK:(pair + 1) * BLOCK] = jnp.where(
            sj == 0, sinks_ref[layer, h], jnp.where(mask, acc, -jnp.inf))


def _bias_table(rel_bias, sinks, bucket_t):
    n_layers = sinks.shape[0]
    smem = pl.BlockSpec(memory_space=pltpu.SMEM)
    return pl.pallas_call(
        _bias_kernel,
        grid=(n_layers, 2, N_KV_HEADS, 2),
        in_specs=[smem, smem, pl.BlockSpec((2 * BLOCK, BLOCK), lambda l, f, k, e: (0, 0))],
        out_specs=pl.BlockSpec((None, None, None, None, 2 * BLOCK, GROUP_PAIRS * BLOCK),
                               lambda l, f, k, e: (l, f, k, e, 0, 0)),
        out_shape=jax.ShapeDtypeStruct((n_layers, 2, N_KV_HEADS, 2, 2 * BLOCK, GROUP_PAIRS * BLOCK), F32),
        compiler_params=_params(*["parallel"] * 4),
        name="bias_table",
    )(rel_bias, sinks, bucket_t)


def _swa_kernel(q_ref, klp_ref, klc_ref, khp_ref, khc_ref, vep_ref, vec_ref, vop_ref, voc_ref,
                bias_ref, o_ref):
    k_keep = jnp.where(lax.broadcasted_iota(jnp.int32, (2 * BLOCK, PAIR_W), 0) == 0, 0.0, 1.0).astype(BF16)
    v_row = lax.broadcasted_iota(jnp.int32, (PAIR_W, 2 * BLOCK), 0)
    v_col = lax.broadcasted_iota(jnp.int32, (PAIR_W, 2 * BLOCK), 1)
    ve_keep = jnp.where((v_col == 0) & (v_row < HEAD_DIM), 0.0, 1.0).astype(BF16)
    vo_keep = jnp.where((v_col == 0) & (v_row >= HEAD_DIM), 0.0, 1.0).astype(BF16)

    for kv in range(N_KV_HEADS):
        tile = slice(kv * PAIR_W, (kv + 1) * PAIR_W)
        klo = jnp.concatenate([klp_ref[:, tile], klc_ref[:, tile]], axis=0) * k_keep
        khi = jnp.concatenate([khp_ref[:, tile], khc_ref[:, tile]], axis=0) * k_keep
        ve = jnp.concatenate([vep_ref[tile, :], vec_ref[tile, :]], axis=1) * ve_keep
        vo = jnp.concatenate([vop_ref[tile, :], voc_ref[tile, :]], axis=1) * vo_keep
        q0 = kv * GROUP_PAIRS * PAIR_W
        q_t = jnp.concatenate([q_ref[q0 + pr * PAIR_W:q0 + (pr + 1) * PAIR_W, :]
                               for pr in range(GROUP_PAIRS)], axis=1)

        s_both = _dot(jnp.concatenate([klo, khi], axis=0), q_t)

        def probs(parity):
            s = s_both[parity * 2 * BLOCK:(parity + 1) * 2 * BLOCK] + bias_ref[kv, parity]
            return jnp.exp(s - jnp.max(s, axis=0, keepdims=True)).astype(BF16)

        oe = _dot(ve, probs(0))
        oo = _dot(vo, probs(1))
        num = jnp.concatenate([oe[:HEAD_DIM], oo[HEAD_DIM:]], axis=0)
        den = jnp.concatenate([oe[HEAD_DIM:], oo[:HEAD_DIM]], axis=0)
        out_t = num / den
        for pr in range(GROUP_PAIRS):
            o_ref[:, q0 + pr * PAIR_W:q0 + (pr + 1) * PAIR_W] = (
                out_t[:, pr * BLOCK:(pr + 1) * BLOCK].T.astype(BF16))


def _swa_attn(q_t, klo, khi, ve_t, vo_t, bias, layer):
    bsz, seq, wide = klo.shape
    nb = seq // BLOCK
    prev = lambda b, n: (b, jnp.maximum(n - 1, 0), 0)
    cur = lambda b, n: (b, n, 0)
    prev_t = lambda b, n: (0, b * nb + jnp.maximum(n - 1, 0))
    cur_t = lambda b, n: (0, b * nb + n)
    k_prev, k_cur = pl.BlockSpec((None, BLOCK, wide), prev), pl.BlockSpec((None, BLOCK, wide), cur)
    v_prev, v_cur = pl.BlockSpec((wide, BLOCK), prev_t), pl.BlockSpec((wide, BLOCK), cur_t)
    return pl.pallas_call(
        _swa_kernel,
        grid=(bsz, nb),
        in_specs=[
            pl.BlockSpec((D_MODEL, BLOCK), cur_t),
            k_prev, k_cur, k_prev, k_cur, v_prev, v_cur, v_prev, v_cur,
            pl.BlockSpec((None, None, N_KV_HEADS, 2, 2 * BLOCK, GROUP_PAIRS * BLOCK),
                         lambda b, n: (layer, jnp.minimum(n, 1), 0, 0, 0, 0)),
        ],
        out_specs=pl.BlockSpec((None, BLOCK, D_MODEL), cur),
        out_shape=jax.ShapeDtypeStruct((bsz, seq, D_MODEL), BF16),
        compiler_params=_params("parallel", "arbitrary"),
        name="swa_attn",
    )(q_t, klo, klo, khi, khi, ve_t, ve_t, vo_t, vo_t, bias)


def _ple_kernel(h_ref, g_ref, wgate_ref, p_ref, wple_ref, o_ref):
    x = h_ref[...]
    gate = jax.nn.sigmoid(_dot(_rms(x, g_ref[...]).astype(BF16), wgate_ref[...]))
    o_ref[...] = x + gate * _dot(p_ref[...].astype(BF16), wple_ref[...])


def _ple(h, g, w_gate_all, p_all, w_ple, layer):
    m = h.shape[0]
    return pl.pallas_call(
        _ple_kernel,
        grid=(m // TM_WIDE,),
        in_specs=[
            pl.BlockSpec((TM_WIDE, D_MODEL), lambda i: (i, 0)),
            pl.BlockSpec((1, D_MODEL), lambda i: (0, 0)),
            pl.BlockSpec((None, D_MODEL, D_MODEL), lambda i: (layer, 0, 0), pipeline_mode=pl.Buffered(1)),
            pl.BlockSpec((None, TM_WIDE, PLE_DIM), lambda i: (layer, i, 0)),
            pl.BlockSpec((PLE_DIM, D_MODEL), lambda i: (0, 0), pipeline_mode=pl.Buffered(1)),
        ],
        out_specs=pl.BlockSpec((TM_WIDE, D_MODEL), lambda i: (i, 0)),
        out_shape=jax.ShapeDtypeStruct((m, D_MODEL), F32),
        compiler_params=_params("parallel"),
        name="ple",
    )(h, g, w_gate_all, p_all, w_ple)


def _t5_bucket(d):
    max_exact = N_BUCKETS // 2
    is_small = d < max_exact
    df = jnp.maximum(d, 1).astype(F32)
    large = max_exact + (jnp.log(df / max_exact) / math.log(MAX_DISTANCE / max_exact)
                         * (N_BUCKETS - max_exact)).astype(jnp.int32)
    large = jnp.minimum(large, N_BUCKETS - 1)
    return jnp.where(is_small, d, large)


def kernel(x, p, norm_g, ffn_wg, ffn_wu, ffn_wd, ple_w, ple_gate_w, gla_w_in, gla_w_a2, gla_b_a,
           gla_norm_g, gla_w_o, kv_norm_g, w_kv, k_norm_g, rel_bias, swa_w_q, q_norm_g, sinks, swa_w_o):
    bsz, seq, d = x.shape
    depth = norm_g.shape[0]
    n_a = gla_w_in.shape[0]
    m = bsz * seq
    row = lambda v: v.reshape(1, -1)

    sj = jnp.arange(2 * BLOCK)[:, None]
    qi = jnp.arange(BLOCK)[None, :]
    bucket_t = _t5_bucket(jnp.maximum(qi + BLOCK - sj, 0)).astype(jnp.int32)
    bias = _bias_table(rel_bias, sinks.astype(F32), bucket_t)

    h = x.reshape(m, d)
    kv_tiles = None
    p_all = p.reshape(depth, m, PLE_DIM)
    ffn_w = (ffn_wg[0, 0].astype(BF16), ffn_wu[0, 0].astype(BF16), ffn_wd[0, 0].astype(BF16))
    w_in_all = w_gate_all = None

    def ffn(h, g, layer, sub, extra_casts=()):
        nxt = (layer, sub + 1) if sub == 0 else (layer + 1, 0)
        next_weights = (ffn_wg, ffn_wu, ffn_wd) + nxt if nxt[0] < depth else None
        return _ffn(h, g, *ffn_w, next_weights=next_weights, extra_casts=extra_casts)

    for i in range(depth):
        if i == n_a:
            klo, khi, ve_t, vo_t = _shared_kv(h, row(kv_norm_g), w_kv.astype(BF16), row(k_norm_g))
            kv_tiles = (klo.reshape(bsz, seq, -1), khi.reshape(bsz, seq, -1), ve_t, vo_t)
        if i == 0:
            h, ffn_w, (w_in_all, w_gate_all) = ffn(h, row(norm_g[i, 0]), i, 0, (gla_w_in, ple_gate_w))
        else:
            h, ffn_w, _ = ffn(h, row(norm_g[i, 0]), i, 0)
        if i < n_a:
            w_a1 = jnp.pad(gla_w_in[i][:, GLA_MAIN:], ((0, 0), (0, V7X_LANES - GLA_RANK))).astype(BF16)
            w_a2 = jnp.pad(gla_w_a2[i], ((0, V7X_LANES - GLA_RANK), (0, 0))).astype(BF16)
            proj, log_a = _gla_proj(h, row(norm_g[i, 1]), w_in_all, i, w_a1, w_a2, row(gla_b_a[i]))
            og = _gla_rec(proj.reshape(bsz, seq, GLA_MAIN), log_a.reshape(bsz, seq, GLA_QK),
                          row(gla_norm_g[i]))
            h = _matmul_res(og.reshape(m, GLA_V), gla_w_o[i].astype(BF16), h)
        else:
            j = i - n_a
            gq_col = (jnp.tile(q_norm_g[j], 2) * (HEAD_DIM ** -0.5)).reshape(PAIR_W, 1)
            q_t = _q_proj(h, row(norm_g[i, 1]), swa_w_q[j].astype(BF16), gq_col)
            o = _swa_attn(q_t, *kv_tiles, bias, j)
            h = _matmul_res(o.reshape(m, d), swa_w_o[j].astype(BF16), h)
        h, ffn_w, _ = ffn(h, row(norm_g[i, 2]), i, 1)
        h = _ple(h, row(norm_g[i, 3]), w_gate_all, p_all, ple_w[i].astype(BF16), i)
    return h.reshape(bsz, seq, d)
```

```python
import functools
import math

import jax
import jax.numpy as jnp
from jax import lax
from jax.experimental import pallas as pl
from jax.experimental.pallas import tpu as pltpu

F32 = jnp.float32
BF16 = jnp.bfloat16

D_MODEL = 2048
D_FF = 5632
FFN_RES = 0.5
PLE_DIM = 256
EPS = 1e-6
GLA_HEADS = 4
GLA_QK = 1024
GLA_V = 2048
GLA_DK = 256
GLA_DV = 512
GLA_RANK = 16
GLA_GATE_NORM = 16.0
GLA_CHUNK = 64
GLA_MAIN = 2 * GLA_QK + 2 * GLA_V
HEAD_DIM = 64
N_Q_HEADS = 32
N_KV_HEADS = 4
GROUP = 8
KV_W = N_KV_HEADS * HEAD_DIM
WINDOW = 128
BLOCK = WINDOW
N_BUCKETS = 32
MAX_DISTANCE = WINDOW

V7X_LANES = 128
V7X_VMEM_BYTES = 64 * 1024 * 1024
VMEM_LIMIT = V7X_VMEM_BYTES - 8 * 1024 * 1024

PAIR_W = 2 * HEAD_DIM
assert PAIR_W == V7X_LANES
N_PAIRS = N_Q_HEADS // 2
GROUP_PAIRS = GROUP // 2

TM = 512
TM_WIDE = 1024
TM_FFN = 1024
TF = 512
TM_PROJ = 1024
TN_PROJ = 1536
GATE_COLS = GLA_QK // (GLA_MAIN // TN_PROJ)
T_GLA = 256


def _params(*sem):
    return pltpu.CompilerParams(dimension_semantics=sem, vmem_limit_bytes=VMEM_LIMIT)


def _rms(x, g):
    return x * lax.rsqrt(jnp.mean(x * x, axis=-1, keepdims=True) + EPS) * g


def _dot(a, b):
    return jnp.dot(a, b, preferred_element_type=F32)


def _dot_nt(a, b):
    return lax.dot_general(a, b, (((1,), (1,)), ((), ())), preferred_element_type=F32)


def _dot_tn(a, b):
    return lax.dot_general(a, b, (((0,), (0,)), ((), ())), preferred_element_type=F32)


def _ffn_kernel(h_hbm, g_ref, wg_ref, wu_ref, wd_ref, *rest, n_casts):
    cast_in, (o_ref, *cast_out), (x_ref, xn_ref, x_sem) = (
        rest[:n_casts], rest[n_casts:2 * n_casts + 1], rest[2 * n_casts + 1:])
    for src, dst in zip(cast_in, cast_out):
        dst[...] = src[...].astype(BF16)
    i = pl.program_id(0)
    j = pl.program_id(1)

    def x_copy(tile):
        rows = pl.ds(pl.multiple_of(tile * TM_FFN, TM_FFN), TM_FFN)
        return pltpu.make_async_copy(h_hbm.at[rows], x_ref, x_sem)

    def swiglu_tile(xn):
        gate = _dot(xn, wg_ref[...])
        up = _dot(xn, wu_ref[...])
        act = (gate * jax.nn.sigmoid(gate) * up * FFN_RES).astype(BF16)
        return _dot(act, wd_ref[...])

    @pl.when(j == 0)
    def _():
        @pl.when(i == 0)
        def _():
            x_copy(0).start()

        x_copy(i).wait()
        x = x_ref[...]
        xn = _rms(x, g_ref[...]).astype(BF16)
        xn_ref[...] = xn
        o_ref[...] = x + swiglu_tile(xn)

    @pl.when(j > 0)
    def _():
        @pl.when((j == 1) & (i + 1 < pl.num_programs(0)))
        def _():
            x_copy(i + 1).start()

        o_ref[...] += swiglu_tile(xn_ref[...])


def _stacked_cast_specs(w, n_i, n_j):
    layers, rows, cols = w.shape
    per_layer = 1 << ((n_j // layers).bit_length() - 1)
    blk = rows // (n_i * per_layer)
    assert blk * n_i * per_layer == rows
    used = layers * per_layer

    def index(i, j):
        jj = jnp.minimum(j, used - 1)
        return (jj // per_layer, i * per_layer + jj % per_layer, 0)

    spec = pl.BlockSpec((None, blk, cols), index)
    return w, spec, spec, jax.ShapeDtypeStruct(w.shape, BF16)


def _ffn(h, g, wg, wu, wd, next_weights=None, extra_casts=()):
    m = h.shape[0]
    n_i, n_j = m // TM_FFN, D_FF // TF
    in_specs = [
        pl.BlockSpec(memory_space=pl.ANY),
        pl.BlockSpec((1, D_MODEL), lambda i, j: (0, 0)),
        pl.BlockSpec((D_MODEL, TF), lambda i, j: (0, j)),
        pl.BlockSpec((D_MODEL, TF), lambda i, j: (0, j)),
        pl.BlockSpec((TF, D_MODEL), lambda i, j: (j, 0)),
    ]
    out_specs = [pl.BlockSpec((TM_FFN, D_MODEL), lambda i, j: (i, 0))]
    out_shape = [jax.ShapeDtypeStruct((m, D_MODEL), F32)]
    casts = []
    if next_weights is not None:
        nwg, nwu, nwd, layer, sub = next_weights
        rows = D_MODEL // n_i
        up_in = pl.BlockSpec((None, None, rows, TF), lambda i, j: (layer, sub, i, j))
        up_out = pl.BlockSpec((rows, TF), lambda i, j: (i, j))
        up_shape = jax.ShapeDtypeStruct((D_MODEL, D_FF), BF16)
        casts += [
            (nwg, up_in, up_out, up_shape),
            (nwu, up_in, up_out, up_shape),
            (nwd, pl.BlockSpec((None, None, TF, rows), lambda i, j: (layer, sub, j, i)),
             pl.BlockSpec((TF, rows), lambda i, j: (j, i)), jax.ShapeDtypeStruct((D_FF, D_MODEL), BF16)),
        ]
    casts += [_stacked_cast_specs(w, n_i, n_j) for w in extra_casts]
    outs = pl.pallas_call(
        functools.partial(_ffn_kernel, n_casts=len(casts)),
        grid=(n_i, n_j),
        in_specs=in_specs + [c[1] for c in casts],
        out_specs=out_specs + [c[2] for c in casts],
        out_shape=out_shape + [c[3] for c in casts],
        scratch_shapes=[
            pltpu.VMEM((TM_FFN, D_MODEL), F32),
            pltpu.VMEM((TM_FFN, D_MODEL), BF16),
            pltpu.SemaphoreType.DMA,
        ],
        compiler_params=_params("arbitrary", "arbitrary"),
        name="ffn",
    )(h, g, wg, wu, wd, *[c[0] for c in casts])
    n_next = 0 if next_weights is None else 3
    return outs[0], tuple(outs[1:1 + n_next]), tuple(outs[1 + n_next:])


def _gla_proj_kernel(h_ref, g_ref, w_ref, wa1_ref, wa2_ref, ba_ref, proj_ref, la_ref, xn_ref, a1_ref):
    def column_step(xn, a1):
        z = _dot(a1, wa2_ref[...]) + ba_ref[...]
        log_sig = jnp.minimum(z, 0.0) - jnp.log1p(jnp.exp(-jnp.abs(z)))
        la_ref[...] = log_sig * (1.0 / GLA_GATE_NORM)
        proj_ref[...] = _dot(xn, w_ref[...]).astype(BF16)

    @pl.when(pl.program_id(1) == 0)
    def _():
        xn = _rms(h_ref[...], g_ref[...]).astype(BF16)
        a1 = _dot(xn, wa1_ref[...]).astype(BF16)
        xn_ref[...] = xn
        a1_ref[...] = a1
        column_step(xn, a1)

    @pl.when(pl.program_id(1) > 0)
    def _():
        column_step(xn_ref[...], a1_ref[...])


def _gla_proj(h, g, w_in_all, layer, w_a1, w_a2, b_a):
    m = h.shape[0]
    return pl.pallas_call(
        _gla_proj_kernel,
        grid=(m // TM_PROJ, GLA_MAIN // TN_PROJ),
        in_specs=[
            pl.BlockSpec((TM_PROJ, D_MODEL), lambda i, j: (i, 0)),
            pl.BlockSpec((1, D_MODEL), lambda i, j: (0, 0)),
            pl.BlockSpec((None, D_MODEL, TN_PROJ), lambda i, j: (layer, 0, j)),
            pl.BlockSpec((D_MODEL, V7X_LANES), lambda i, j: (0, 0)),
            pl.BlockSpec((V7X_LANES, GATE_COLS), lambda i, j: (0, j)),
            pl.BlockSpec((1, GATE_COLS), lambda i, j: (0, j)),
        ],
        out_specs=[
            pl.BlockSpec((TM_PROJ, TN_PROJ), lambda i, j: (i, j)),
            pl.BlockSpec((TM_PROJ, GATE_COLS), lambda i, j: (i, j)),
        ],
        out_shape=[
            jax.ShapeDtypeStruct((m, GLA_MAIN), BF16),
            jax.ShapeDtypeStruct((m, GLA_QK), F32),
        ],
        scratch_shapes=[pltpu.VMEM((TM_PROJ, D_MODEL), BF16), pltpu.VMEM((TM_PROJ, V7X_LANES), BF16)],
        compiler_params=_params("parallel", "arbitrary"),
        name="gla_proj",
    )(h, g, w_in_all, w_a1, w_a2, b_a)


def _gla_rec_kernel(q_ref, k_ref, v_ref, r_ref, la_ref, go_ref, o_ref, st_ref):
    c_sz = GLA_CHUNK
    bsz = q_ref.shape[0]

    @pl.when(pl.program_id(0) == 0)
    def _():
        st_ref[...] = jnp.zeros_like(st_ref)

    row = lax.broadcasted_iota(jnp.int32, (c_sz, c_sz), 0)
    col = lax.broadcasted_iota(jnp.int32, (c_sz, c_sz), 1)
    causal = col <= row
    tril = jnp.where(causal, 1.0, 0.0).astype(BF16)
    go = go_ref[...]

    def chunk(c, carry):
        sl = pl.ds(pl.multiple_of(c * c_sz, c_sz), c_sz)
        for bi in range(bsz):
            la = la_ref[bi, sl, :]
            la1 = la.astype(BF16)
            rem = la - la1.astype(F32)
            la2 = rem.astype(BF16)
            la3 = (rem - la2.astype(F32)).astype(BF16)
            b_all = _dot(tril, la1) + _dot(tril, la2) + _dot(tril, la3)
            for h in range(GLA_HEADS):
                kc = slice(h * GLA_DK, (h + 1) * GLA_DK)
                vc = slice(h * GLA_DV, (h + 1) * GLA_DV)
                b = b_all[:, kc]
                b_last = b[c_sz - 1:c_sz, :]
                q = q_ref[bi, sl, kc].astype(F32) * (GLA_DK ** -0.5)
                k = k_ref[bi, sl, kc].astype(F32)
                v = v_ref[bi, sl, vc]
                q_dec = (q * jnp.exp(b)).astype(BF16)
                k_inv = (k * jnp.exp(-b)).astype(BF16)
                k_dec = (k * jnp.exp(b_last - b)).astype(BF16)
                attn = jnp.where(causal, _dot_nt(q_dec, k_inv), 0.0)
                st = st_ref[bi, h]
                o = _dot(attn.astype(BF16), v) + _dot_nt(q_dec, st.astype(BF16))
                st_ref[bi, h] = st * jnp.exp(b_last) + _dot_tn(v, k_dec)
                r = r_ref[bi, sl, vc].astype(F32)
                o_ref[bi, sl, vc] = (_rms(o, go) * (r * jax.nn.sigmoid(r))).astype(BF16)
        return carry

    lax.fori_loop(0, T_GLA // c_sz, chunk, 0)


def _gla_rec(proj, log_a, g_o):
    bsz, seq, _ = proj.shape
    v_blk = 2 * GLA_QK // GLA_V
    return pl.pallas_call(
        _gla_rec_kernel,
        grid=(seq // T_GLA,),
        in_specs=[
            pl.BlockSpec((bsz, T_GLA, GLA_QK), lambda t: (0, t, 0)),
            pl.BlockSpec((bsz, T_GLA, GLA_QK), lambda t: (0, t, 1)),
            pl.BlockSpec((bsz, T_GLA, GLA_V), lambda t: (0, t, v_blk)),
            pl.BlockSpec((bsz, T_GLA, GLA_V), lambda t: (0, t, v_blk + 1)),
            pl.BlockSpec((bsz, T_GLA, GLA_QK), lambda t: (0, t, 0)),
            pl.BlockSpec((1, GLA_DV), lambda t: (0, 0)),
        ],
        out_specs=pl.BlockSpec((bsz, T_GLA, GLA_V), lambda t: (0, t, 0)),
        out_shape=jax.ShapeDtypeStruct((bsz, seq, GLA_V), BF16),
        scratch_shapes=[pltpu.VMEM((bsz, GLA_HEADS, GLA_DV, GLA_DK), F32)],
        compiler_params=_params("arbitrary"),
        name="gla_rec",
    )(proj, proj, proj, proj, log_a, g_o)


def _matmul_res_kernel(a_ref, w_ref, res_ref, o_ref):
    o_ref[...] = res_ref[...] + _dot(a_ref[...], w_ref[...])


def _matmul_res(a, w, res):
    m, kdim = a.shape
    n = w.shape[1]
    return pl.pallas_call(
        _matmul_res_kernel,
        grid=(m // TM_WIDE,),
        in_specs=[
            pl.BlockSpec((TM_WIDE, kdim), lambda i: (i, 0)),
            pl.BlockSpec((kdim, n), lambda i: (0, 0), pipeline_mode=pl.Buffered(1)),
            pl.BlockSpec((TM_WIDE, n), lambda i: (i, 0)),
        ],
        out_specs=pl.BlockSpec((TM_WIDE, n), lambda i: (i, 0)),
        out_shape=jax.ShapeDtypeStruct((m, n), F32),
        compiler_params=_params("parallel"),
        name="matmul_res",
    )(a, w, res)


def _q_proj_kernel(h_ref, g_ref, w_ref, gq_ref, o_ref):
    q = _dot(_rms(h_ref[...], g_ref[...]).astype(BF16), w_ref[...])
    tm = q.shape[0]
    gq = jnp.broadcast_to(gq_ref[...], (PAIR_W, tm))
    for pr in range(N_PAIRS):
        t = q[:, pr * PAIR_W:(pr + 1) * PAIR_W].T
        halves = []
        for hd in range(2):
            th = t[hd * HEAD_DIM:(hd + 1) * HEAD_DIM]
            ms = jnp.mean(th * th, axis=0, keepdims=True)
            halves.append(th * lax.rsqrt(ms + EPS))
        o_ref[pr * PAIR_W:(pr + 1) * PAIR_W, :] = (jnp.concatenate(halves, axis=0) * gq).astype(BF16)


def _q_proj(h, g, w, gq_col):
    m = h.shape[0]
    return pl.pallas_call(
        _q_proj_kernel,
        grid=(m // TM,),
        in_specs=[
            pl.BlockSpec((TM, D_MODEL), lambda i: (i, 0)),
            pl.BlockSpec((1, D_MODEL), lambda i: (0, 0)),
            pl.BlockSpec((D_MODEL, D_MODEL), lambda i: (0, 0)),
            pl.BlockSpec((PAIR_W, 1), lambda i: (0, 0)),
        ],
        out_specs=pl.BlockSpec((D_MODEL, TM), lambda i: (0, i)),
        out_shape=jax.ShapeDtypeStruct((D_MODEL, m), BF16),
        compiler_params=_params("parallel"),
        name="q_proj",
    )(h, g, w, gq_col)


def _kv_kernel(h_ref, g_ref, w_ref, gk_ref, klo_ref, khi_ref, ve_ref, vo_ref):
    kv = _dot(_rms(h_ref[...], g_ref[...]).astype(BF16), w_ref[...])
    tm = kv.shape[0]
    gk = gk_ref[...]
    zeros = jnp.zeros((tm, HEAD_DIM), F32)
    ones_t = jnp.ones((HEAD_DIM, tm), F32)
    klo, khi = [], []
    for i in range(N_KV_HEADS):
        k = _rms(kv[:, i * HEAD_DIM:(i + 1) * HEAD_DIM], gk)
        klo += [k, zeros]
        khi += [zeros, k]
    klo_ref[...] = jnp.concatenate(klo, axis=1).astype(BF16)
    khi_ref[...] = jnp.concatenate(khi, axis=1).astype(BF16)
    for i in range(N_KV_HEADS // 2):
        vt = kv[:, KV_W + i * PAIR_W:KV_W + (i + 1) * PAIR_W].T
        for hd in range(2):
            v_t = vt[hd * HEAD_DIM:(hd + 1) * HEAD_DIM]
            rows = slice((2 * i + hd) * PAIR_W, (2 * i + hd + 1) * PAIR_W)
            ve_ref[rows, :] = jnp.concatenate([v_t, ones_t], axis=0).astype(BF16)
            vo_ref[rows, :] = jnp.concatenate([ones_t, v_t], axis=0).astype(BF16)


def _shared_kv(h, g, w, gk):
    m = h.shape[0]
    wide = N_KV_HEADS * PAIR_W
    return pl.pallas_call(
        _kv_kernel,
        grid=(m // TM,),
        in_specs=[
            pl.BlockSpec((TM, D_MODEL), lambda i: (i, 0)),
            pl.BlockSpec((1, D_MODEL), lambda i: (0, 0)),
            pl.BlockSpec((D_MODEL, 2 * KV_W), lambda i: (0, 0)),
            pl.BlockSpec((1, HEAD_DIM), lambda i: (0, 0)),
        ],
        out_specs=[pl.BlockSpec((TM, wide), lambda i: (i, 0))] * 2
        + [pl.BlockSpec((wide, TM), lambda i: (0, i))] * 2,
        out_shape=[jax.ShapeDtypeStruct((m, wide), BF16)] * 2 + [jax.ShapeDtypeStruct((wide, m), BF16)] * 2,
        compiler_params=_params("parallel"),
        name="shared_kv",
    )(h, g, w, gk)


def _bias_kernel(rb_ref, sinks_ref, bucket_ref, o_ref):
    layer, later, kv, parity = (pl.program_id(a) for a in range(4))
    bucket = bucket_ref[...]
    sj = lax.broadcasted_iota(jnp.int32, bucket.shape, 0)
    qi = lax.broadcasted_iota(jnp.int32, bucket.shape, 1)
    dist = qi + BLOCK - sj
    kpos = later * BLOCK - BLOCK + sj
    mask = (dist >= 0) & (dist < WINDOW) & (kpos >= 0)
    for pair in range(GROUP_PAIRS):
        h = kv * GROUP + 2 * pair + parity
        acc = jnp.zeros(bucket.shape, F32)
        for b in range(N_BUCKETS):
            acc = jnp.where(bucket == b, rb_ref[b, h], acc)
        o_ref[:, pair * BLOCK:(pair + 1) * BLOCK] = jnp.where(
            sj == 0, sinks_ref[layer, h], jnp.where(mask, acc, -jnp.inf))


def _bias_table(rel_bias, sinks, bucket_t):
    n_layers = sinks.shape[0]
    smem = pl.BlockSpec(memory_space=pltpu.SMEM)
    return pl.pallas_call(
        _bias_kernel,
        grid=(n_layers, 2, N_KV_HEADS, 2),
        in_specs=[smem, smem, pl.BlockSpec((2 * BLOCK, BLOCK), lambda l, f, k, e: (0, 0))],
        out_specs=pl.BlockSpec((None, None, None, None, 2 * BLOCK, GROUP_PAIRS * BLOCK),
                               lambda l, f, k, e: (l, f, k, e, 0, 0)),
        out_shape=jax.ShapeDtypeStruct((n_layers, 2, N_KV_HEADS, 2, 2 * BLOCK, GROUP_PAIRS * BLOCK), F32),
        compiler_params=_params(*["parallel"] * 4),
        name="bias_table",
    )(rel_bias, sinks, bucket_t)


def _swa_kernel(q_ref, klp_ref, klc_ref, khp_ref, khc_ref, vep_ref, vec_ref, vop_ref, voc_ref,
                bias_ref, o_ref):
    k_keep = jnp.where(lax.broadcasted_iota(jnp.int32, (2 * BLOCK, PAIR_W), 0) == 0, 0.0, 1.0).astype(BF16)
    v_row = lax.broadcasted_iota(jnp.int32, (PAIR_W, 2 * BLOCK), 0)
    v_col = lax.broadcasted_iota(jnp.int32, (PAIR_W, 2 * BLOCK), 1)
    ve_keep = jnp.where((v_col == 0) & (v_row < HEAD_DIM), 0.0, 1.0).astype(BF16)
    vo_keep = jnp.where((v_col == 0) & (v_row >= HEAD_DIM), 0.0, 1.0).astype(BF16)

    for kv in range(N_KV_HEADS):
        tile = slice(kv * PAIR_W, (kv + 1) * PAIR_W)
        klo = jnp.concatenate([klp_ref[:, tile], klc_ref[:, tile]], axis=0) * k_keep
        khi = jnp.concatenate([khp_ref[:, tile], khc_ref[:, tile]], axis=0) * k_keep
        ve = jnp.concatenate([vep_ref[tile, :], vec_ref[tile, :]], axis=1) * ve_keep
        vo = jnp.concatenate([vop_ref[tile, :], voc_ref[tile, :]], axis=1) * vo_keep
        q0 = kv * GROUP_PAIRS * PAIR_W
        q_t = jnp.concatenate([q_ref[q0 + pr * PAIR_W:q0 + (pr + 1) * PAIR_W, :]
                               for pr in range(GROUP_PAIRS)], axis=1)

        s_both = _dot(jnp.concatenate([klo, khi], axis=0), q_t)

        def probs(parity):
            s = s_both[parity * 2 * BLOCK:(parity + 1) * 2 * BLOCK] + bias_ref[kv, parity]
            return jnp.exp(s - jnp.max(s, axis=0, keepdims=True)).astype(BF16)

        oe = _dot(ve, probs(0))
        oo = _dot(vo, probs(1))
        num = jnp.concatenate([oe[:HEAD_DIM], oo[HEAD_DIM:]], axis=0)
        den = jnp.concatenate([oe[HEAD_DIM:], oo[:HEAD_DIM]], axis=0)
        out_t = num / den
        for pr in range(GROUP_PAIRS):
            o_ref[:, q0 + pr * PAIR_W:q0 + (pr + 1) * PAIR_W] = (
                out_t[:, pr * BLOCK:(pr + 1) * BLOCK].T.astype(BF16))


def _swa_attn(q_t, klo, khi, ve_t, vo_t, bias, layer):
    bsz, seq, wide = klo.shape
    nb = seq // BLOCK
    prev = lambda b, n: (b, jnp.maximum(n - 1, 0), 0)
    cur = lambda b, n: (b, n, 0)
    prev_t = lambda b, n: (0, b * nb + jnp.maximum(n - 1, 0))
    cur_t = lambda b, n: (0, b * nb + n)
    k_prev, k_cur = pl.BlockSpec((None, BLOCK, wide), prev), pl.BlockSpec((None, BLOCK, wide), cur)
    v_prev, v_cur = pl.BlockSpec((wide, BLOCK), prev_t), pl.BlockSpec((wide, BLOCK), cur_t)
    return pl.pallas_call(
        _swa_kernel,
        grid=(bsz, nb),
        in_specs=[
            pl.BlockSpec((D_MODEL, BLOCK), cur_t),
            k_prev, k_cur, k_prev, k_cur, v_prev, v_cur, v_prev, v_cur,
            pl.BlockSpec((None, None, N_KV_HEADS, 2, 2 * BLOCK, GROUP_PAIRS * BLOCK),
                         lambda b, n: (layer, jnp.minimum(n, 1), 0, 0, 0, 0)),
        ],
        out_specs=pl.BlockSpec((None, BLOCK, D_MODEL), cur),
        out_shape=jax.ShapeDtypeStruct((bsz, seq, D_MODEL), BF16),
        compiler_params=_params("parallel", "arbitrary"),
        name="swa_attn",
    )(q_t, klo, klo, khi, khi, ve_t, ve_t, vo_t, vo_t, bias)


def _ple_kernel(h_ref, g_ref, wgate_ref, p_ref, wple_ref, o_ref):
    x = h_ref[...]
    gate = jax.nn.sigmoid(_dot(_rms(x, g_ref[...]).astype(BF16), wgate_ref[...]))
    o_ref[...] = x + gate * _dot(p_ref[...].astype(BF16), wple_ref[...])


def _ple(h, g, w_gate_all, p_all, w_ple, layer):
    m = h.shape[0]
    return pl.pallas_call(
        _ple_kernel,
        grid=(m // TM_WIDE,),
        in_specs=[
            pl.BlockSpec((TM_WIDE, D_MODEL), lambda i: (i, 0)),
            pl.BlockSpec((1, D_MODEL), lambda i: (0, 0)),
            pl.BlockSpec((None, D_MODEL, D_MODEL), lambda i: (layer, 0, 0), pipeline_mode=pl.Buffered(1)),
            pl.BlockSpec((None, TM_WIDE, PLE_DIM), lambda i: (layer, i, 0)),
            pl.BlockSpec((PLE_DIM, D_MODEL), lambda i: (0, 0), pipeline_mode=pl.Buffered(1)),
        ],
        out_specs=pl.BlockSpec((TM_WIDE, D_MODEL), lambda i: (i, 0)),
        out_shape=jax.ShapeDtypeStruct((m, D_MODEL), F32),
        compiler_params=_params("parallel"),
        name="ple",
    )(h, g, w_gate_all, p_all, w_ple)


def _t5_bucket(d):
    max_exact = N_BUCKETS // 2
    is_small = d < max_exact
    df = jnp.maximum(d, 1).astype(F32)
    large = max_exact + (jnp.log(df / max_exact) / math.log(MAX_DISTANCE / max_exact)
                         * (N_BUCKETS - max_exact)).astype(jnp.int32)
    large = jnp.minimum(large, N_BUCKETS - 1)
    return jnp.where(is_small, d, large)


def kernel(x, p, norm_g, ffn_wg, ffn_wu, ffn_wd, ple_w, ple_gate_w, gla_w_in, gla_w_a2, gla_b_a,
           gla_norm_g, gla_w_o, kv_norm_g, w_kv, k_norm_g, rel_bias, swa_w_q, q_norm_g, sinks, swa_w_o):
    bsz, seq, d = x.shape
    depth = norm_g.shape[0]
    n_a = gla_w_in.shape[0]
    m = bsz * seq
    row = lambda v: v.reshape(1, -1)

    sj = jnp.arange(2 * BLOCK)[:, None]
    qi = jnp.arange(BLOCK)[None, :]
    bucket_t = _t5_bucket(jnp.maximum(qi + BLOCK - sj, 0)).astype(jnp.int32)
    bias = _bias_table(rel_bias, sinks.astype(F32), bucket_t)

    h = x.reshape(m, d)
    kv_tiles = None
    p_all = p.reshape(depth, m, PLE_DIM)
    ffn_w = (ffn_wg[0, 0].astype(BF16), ffn_wu[0, 0].astype(BF16), ffn_wd[0, 0].astype(BF16))
    w_in_all = gla_w_in.astype(BF16)
    w_a1_all = jnp.pad(w_in_all[:, :, GLA_MAIN:], ((0, 0), (0, 0), (0, V7X_LANES - GLA_RANK)))
    w_a2_all = jnp.pad(gla_w_a2, ((0, 0), (0, V7X_LANES - GLA_RANK), (0, 0))).astype(BF16)
    w_gate_all = None

    def ffn(h, g, layer, sub, extra_casts=()):
        nxt = (layer, sub + 1) if sub == 0 else (layer + 1, 0)
        next_weights = (ffn_wg, ffn_wu, ffn_wd) + nxt if nxt[0] < depth else None
        return _ffn(h, g, *ffn_w, next_weights=next_weights, extra_casts=extra_casts)

    for i in range(depth):
        if i == n_a:
            klo, khi, ve_t, vo_t = _shared_kv(h, row(kv_norm_g), w_kv.astype(BF16), row(k_norm_g))
            kv_tiles = (klo.reshape(bsz, seq, -1), khi.reshape(bsz, seq, -1), ve_t, vo_t)
        if i == 0:
            h, ffn_w, (w_gate_all,) = ffn(h, row(norm_g[i, 0]), i, 0, (ple_gate_w,))
        else:
            h, ffn_w, _ = ffn(h, row(norm_g[i, 0]), i, 0)
        if i < n_a:
            proj, log_a = _gla_proj(h, row(norm_g[i, 1]), w_in_all, i, w_a1_all[i], w_a2_all[i],
                                    row(gla_b_a[i]))
            og = _gla_rec(proj.reshape(bsz, seq, GLA_MAIN), log_a.reshape(bsz, seq, GLA_QK),
                          row(gla_norm_g[i]))
            h = _matmul_res(og.reshape(m, GLA_V), gla_w_o[i].astype(BF16), h)
        else:
            j = i - n_a
            gq_col = (jnp.tile(q_norm_g[j], 2) * (HEAD_DIM ** -0.5)).reshape(PAIR_W, 1)
            q_t = _q_proj(h, row(norm_g[i, 1]), swa_w_q[j].astype(BF16), gq_col)
            o = _swa_attn(q_t, *kv_tiles, bias, j)
            h = _matmul_res(o.reshape(m, d), swa_w_o[j].astype(BF16), h)
        h, ffn_w, _ = ffn(h, row(norm_g[i, 2]), i, 1)
        h = _ple(h, row(norm_g[i, 3]), w_gate_all, p_all, ple_w[i].astype(BF16), i)
    return h.reshape(bsz, seq, d)
```

```python
import functools
import math

import jax
import jax.numpy as jnp
from jax import lax
from jax.experimental import pallas as pl
from jax.experimental.pallas import tpu as pltpu

F32 = jnp.float32
BF16 = jnp.bfloat16

D_MODEL = 2048
D_FF = 5632
FFN_RES = 0.5
PLE_DIM = 256
EPS = 1e-6
GLA_HEADS = 4
GLA_QK = 1024
GLA_V = 2048
GLA_DK = 256
GLA_DV = 512
GLA_RANK = 16
GLA_GATE_NORM = 16.0
GLA_CHUNK = 64
GLA_MAIN = 2 * GLA_QK + 2 * GLA_V
HEAD_DIM = 64
N_Q_HEADS = 32
N_KV_HEADS = 4
GROUP = 8
KV_W = N_KV_HEADS * HEAD_DIM
WINDOW = 128
BLOCK = WINDOW
N_BUCKETS = 32
MAX_DISTANCE = WINDOW

V7X_LANES = 128
V7X_VMEM_BYTES = 64 * 1024 * 1024
VMEM_LIMIT = V7X_VMEM_BYTES - 8 * 1024 * 1024

PAIR_W = 2 * HEAD_DIM
assert PAIR_W == V7X_LANES
N_PAIRS = N_Q_HEADS // 2
GROUP_PAIRS = GROUP // 2

TM = 512
TM_WIDE = 1024
TM_FFN = 1024
TF = 512
TM_PROJ = 1024
TN_PROJ = 1536
GATE_COLS = GLA_QK // (GLA_MAIN // TN_PROJ)
T_GLA = 256


def _params(*sem):
    return pltpu.CompilerParams(dimension_semantics=sem, vmem_limit_bytes=VMEM_LIMIT)


def _rms(x, g):
    return x * lax.rsqrt(jnp.mean(x * x, axis=-1, keepdims=True) + EPS) * g


def _dot(a, b):
    return jnp.dot(a, b, preferred_element_type=F32)


def _dot_nt(a, b):
    return lax.dot_general(a, b, (((1,), (1,)), ((), ())), preferred_element_type=F32)


def _dot_tn(a, b):
    return lax.dot_general(a, b, (((0,), (0,)), ((), ())), preferred_element_type=F32)


def _ffn_kernel(h_hbm, g_ref, wg_ref, wu_ref, wd_ref, *rest, n_casts):
    cast_in, (o_ref, *cast_out), (x_ref, xn_ref, x_sem) = (
        rest[:n_casts], rest[n_casts:2 * n_casts + 1], rest[2 * n_casts + 1:])
    for src, dst in zip(cast_in, cast_out):
        dst[...] = src[...].astype(BF16)
    i = pl.program_id(0)
    j = pl.program_id(1)

    def x_copy(tile):
        rows = pl.ds(pl.multiple_of(tile * TM_FFN, TM_FFN), TM_FFN)
        return pltpu.make_async_copy(h_hbm.at[rows], x_ref, x_sem)

    def swiglu_tile(xn):
        gate = _dot(xn, wg_ref[...])
        up = _dot(xn, wu_ref[...])
        act = (gate * jax.nn.sigmoid(gate) * up * FFN_RES).astype(BF16)
        return _dot(act, wd_ref[...])

    @pl.when(j == 0)
    def _():
        @pl.when(i == 0)
        def _():
            x_copy(0).start()

        x_copy(i).wait()
        x = x_ref[...]
        xn = _rms(x, g_ref[...]).astype(BF16)
        xn_ref[...] = xn
        o_ref[...] = x + swiglu_tile(xn)

    @pl.when(j > 0)
    def _():
        @pl.when((j == 1) & (i + 1 < pl.num_programs(0)))
        def _():
            x_copy(i + 1).start()

        o_ref[...] += swiglu_tile(xn_ref[...])


def _stacked_cast_specs(w, n_i, n_j):
    layers, rows, cols = w.shape
    per_layer = 1 << ((n_j // layers).bit_length() - 1)
    blk = rows // (n_i * per_layer)
    assert blk * n_i * per_layer == rows
    used = layers * per_layer

    def index(i, j):
        jj = jnp.minimum(j, used - 1)
        return (jj // per_layer, i * per_layer + jj % per_layer, 0)

    spec = pl.BlockSpec((None, blk, cols), index)
    return w, spec, spec, jax.ShapeDtypeStruct(w.shape, BF16)


def _ffn(h, g, wg, wu, wd, next_weights=None, extra_casts=()):
    m = h.shape[0]
    n_i, n_j = m // TM_FFN, D_FF // TF
    in_specs = [
        pl.BlockSpec(memory_space=pl.ANY),
        pl.BlockSpec((1, D_MODEL), lambda i, j: (0, 0)),
        pl.BlockSpec((D_MODEL, TF), lambda i, j: (0, j)),
        pl.BlockSpec((D_MODEL, TF), lambda i, j: (0, j)),
        pl.BlockSpec((TF, D_MODEL), lambda i, j: (j, 0)),
    ]
    out_specs = [pl.BlockSpec((TM_FFN, D_MODEL), lambda i, j: (i, 0))]
    out_shape = [jax.ShapeDtypeStruct((m, D_MODEL), F32)]
    casts = []
    if next_weights is not None:
        nwg, nwu, nwd, layer, sub = next_weights
        rows = D_MODEL // n_i
        up_in = pl.BlockSpec((None, None, rows, TF), lambda i, j: (layer, sub, i, j))
        up_out = pl.BlockSpec((rows, TF), lambda i, j: (i, j))
        up_shape = jax.ShapeDtypeStruct((D_MODEL, D_FF), BF16)
        casts += [
            (nwg, up_in, up_out, up_shape),
            (nwu, up_in, up_out, up_shape),
            (nwd, pl.BlockSpec((None, None, TF, rows), lambda i, j: (layer, sub, j, i)),
             pl.BlockSpec((TF, rows), lambda i, j: (j, i)), jax.ShapeDtypeStruct((D_FF, D_MODEL), BF16)),
        ]
    casts += [_stacked_cast_specs(w, n_i, n_j) for w in extra_casts]
    outs = pl.pallas_call(
        functools.partial(_ffn_kernel, n_casts=len(casts)),
        grid=(n_i, n_j),
        in_specs=in_specs + [c[1] for c in casts],
        out_specs=out_specs + [c[2] for c in casts],
        out_shape=out_shape + [c[3] for c in casts],
        scratch_shapes=[
            pltpu.VMEM((TM_FFN, D_MODEL), F32),
            pltpu.VMEM((TM_FFN, D_MODEL), BF16),
            pltpu.SemaphoreType.DMA,
        ],
        compiler_params=_params("arbitrary", "arbitrary"),
        name="ffn",
    )(h, g, wg, wu, wd, *[c[0] for c in casts])
    n_next = 0 if next_weights is None else 3
    return outs[0], tuple(outs[1:1 + n_next]), tuple(outs[1 + n_next:])


def _gla_proj_kernel(h_ref, g_ref, w_ref, wa1_ref, wa2_ref, ba_ref, proj_ref, la_ref, xn_ref, a1_ref):
    def column_step(xn, a1):
        z = _dot(a1, wa2_ref[...]) + ba_ref[...]
        log_sig = jnp.minimum(z, 0.0) - jnp.log1p(jnp.exp(-jnp.abs(z)))
        la_ref[...] = log_sig * (1.0 / GLA_GATE_NORM)
        proj_ref[...] = _dot_nt(xn, w_ref[...]).astype(BF16)

    @pl.when(pl.program_id(1) == 0)
    def _():
        xn = _rms(h_ref[...], g_ref[...]).astype(BF16)
        a1 = _dot(xn, wa1_ref[...]).astype(BF16)
        xn_ref[...] = xn
        a1_ref[...] = a1
        column_step(xn, a1)

    @pl.when(pl.program_id(1) > 0)
    def _():
        column_step(xn_ref[...], a1_ref[...])


def _gla_proj(h, g, w_in_all, layer, w_a1, w_a2, b_a):
    m = h.shape[0]
    return pl.pallas_call(
        _gla_proj_kernel,
        grid=(m // TM_PROJ, GLA_MAIN // TN_PROJ),
        in_specs=[
            pl.BlockSpec((TM_PROJ, D_MODEL), lambda i, j: (i, 0)),
            pl.BlockSpec((1, D_MODEL), lambda i, j: (0, 0)),
            pl.BlockSpec((None, TN_PROJ, D_MODEL), lambda i, j: (layer, j, 0)),
            pl.BlockSpec((D_MODEL, V7X_LANES), lambda i, j: (0, 0)),
            pl.BlockSpec((V7X_LANES, GATE_COLS), lambda i, j: (0, j)),
            pl.BlockSpec((1, GATE_COLS), lambda i, j: (0, j)),
        ],
        out_specs=[
            pl.BlockSpec((TM_PROJ, TN_PROJ), lambda i, j: (i, j)),
            pl.BlockSpec((TM_PROJ, GATE_COLS), lambda i, j: (i, j)),
        ],
        out_shape=[
            jax.ShapeDtypeStruct((m, GLA_MAIN), BF16),
            jax.ShapeDtypeStruct((m, GLA_QK), F32),
        ],
        scratch_shapes=[pltpu.VMEM((TM_PROJ, D_MODEL), BF16), pltpu.VMEM((TM_PROJ, V7X_LANES), BF16)],
        compiler_params=_params("parallel", "arbitrary"),
        name="gla_proj",
    )(h, g, w_in_all, w_a1, w_a2, b_a)


def _gla_rec_kernel(q_ref, k_ref, v_ref, r_ref, la_ref, go_ref, o_ref, st_ref):
    c_sz = GLA_CHUNK
    bsz = q_ref.shape[0]

    @pl.when(pl.program_id(0) == 0)
    def _():
        st_ref[...] = jnp.zeros_like(st_ref)

    row = lax.broadcasted_iota(jnp.int32, (c_sz, c_sz), 0)
    col = lax.broadcasted_iota(jnp.int32, (c_sz, c_sz), 1)
    causal = col <= row
    tril = jnp.where(causal, 1.0, 0.0).astype(BF16)
    go = go_ref[...]

    def chunk(c, carry):
        sl = pl.ds(pl.multiple_of(c * c_sz, c_sz), c_sz)
        for bi in range(bsz):
            la = la_ref[bi, sl, :]
            la1 = la.astype(BF16)
            rem = la - la1.astype(F32)
            la2 = rem.astype(BF16)
            la3 = (rem - la2.astype(F32)).astype(BF16)
            b_all = _dot(tril, la1) + _dot(tril, la2) + _dot(tril, la3)
            for h in range(GLA_HEADS):
                kc = slice(h * GLA_DK, (h + 1) * GLA_DK)
                vc = slice(h * GLA_DV, (h + 1) * GLA_DV)
                b = b_all[:, kc]
                b_last = b[c_sz - 1:c_sz, :]
                q = q_ref[bi, sl, kc].astype(F32) * (GLA_DK ** -0.5)
                k = k_ref[bi, sl, kc].astype(F32)
                v = v_ref[bi, sl, vc]
                q_dec = (q * jnp.exp(b)).astype(BF16)
                k_inv = (k * jnp.exp(-b)).astype(BF16)
                k_dec = (k * jnp.exp(b_last - b)).astype(BF16)
                attn = jnp.where(causal, _dot_nt(q_dec, k_inv), 0.0)
                st = st_ref[bi, h]
                o = _dot(attn.astype(BF16), v) + _dot_nt(q_dec, st.astype(BF16))
                st_ref[bi, h] = st * jnp.exp(b_last) + _dot_tn(v, k_dec)
                r = r_ref[bi, sl, vc].astype(F32)
                o_ref[bi, sl, vc] = (_rms(o, go) * (r * jax.nn.sigmoid(r))).astype(BF16)
        return carry

    lax.fori_loop(0, T_GLA // c_sz, chunk, 0)


def _gla_rec(proj, log_a, g_o):
    bsz, seq, _ = proj.shape
    v_blk = 2 * GLA_QK // GLA_V
    return pl.pallas_call(
        _gla_rec_kernel,
        grid=(seq // T_GLA,),
        in_specs=[
            pl.BlockSpec((bsz, T_GLA, GLA_QK), lambda t: (0, t, 0)),
            pl.BlockSpec((bsz, T_GLA, GLA_QK), lambda t: (0, t, 1)),
            pl.BlockSpec((bsz, T_GLA, GLA_V), lambda t: (0, t, v_blk)),
            pl.BlockSpec((bsz, T_GLA, GLA_V), lambda t: (0, t, v_blk + 1)),
            pl.BlockSpec((bsz, T_GLA, GLA_QK), lambda t: (0, t, 0)),
            pl.BlockSpec((1, GLA_DV), lambda t: (0, 0)),
        ],
        out_specs=pl.BlockSpec((bsz, T_GLA, GLA_V), lambda t: (0, t, 0)),
        out_shape=jax.ShapeDtypeStruct((bsz, seq, GLA_V), BF16),
        scratch_shapes=[pltpu.VMEM((bsz, GLA_HEADS, GLA_DV, GLA_DK), F32)],
        compiler_params=_params("arbitrary"),
        name="gla_rec",
    )(proj, proj, proj, proj, log_a, g_o)


def _matmul_res_kernel(a_ref, w_ref, res_ref, o_ref):
    o_ref[...] = res_ref[...] + _dot(a_ref[...], w_ref[...])


def _matmul_res(a, w, res):
    m, kdim = a.shape
    n = w.shape[1]
    return pl.pallas_call(
        _matmul_res_kernel,
        grid=(m // TM_WIDE,),
        in_specs=[
            pl.BlockSpec((TM_WIDE, kdim), lambda i: (i, 0)),
            pl.BlockSpec((kdim, n), lambda i: (0, 0), pipeline_mode=pl.Buffered(1)),
            pl.BlockSpec((TM_WIDE, n), lambda i: (i, 0)),
        ],
        out_specs=pl.BlockSpec((TM_WIDE, n), lambda i: (i, 0)),
        out_shape=jax.ShapeDtypeStruct((m, n), F32),
        compiler_params=_params("parallel"),
        name="matmul_res",
    )(a, w, res)


def _q_proj_kernel(h_ref, g_ref, w_ref, gq_ref, o_ref):
    q = _dot(_rms(h_ref[...], g_ref[...]).astype(BF16), w_ref[...])
    tm = q.shape[0]
    gq = jnp.broadcast_to(gq_ref[...], (PAIR_W, tm))
    for pr in range(N_PAIRS):
        t = q[:, pr * PAIR_W:(pr + 1) * PAIR_W].T
        halves = []
        for hd in range(2):
            th = t[hd * HEAD_DIM:(hd + 1) * HEAD_DIM]
            ms = jnp.mean(th * th, axis=0, keepdims=True)
            halves.append(th * lax.rsqrt(ms + EPS))
        o_ref[pr * PAIR_W:(pr + 1) * PAIR_W, :] = (jnp.concatenate(halves, axis=0) * gq).astype(BF16)


def _q_proj(h, g, w, gq_col):
    m = h.shape[0]
    return pl.pallas_call(
        _q_proj_kernel,
        grid=(m // TM,),
        in_specs=[
            pl.BlockSpec((TM, D_MODEL), lambda i: (i, 0)),
            pl.BlockSpec((1, D_MODEL), lambda i: (0, 0)),
            pl.BlockSpec((D_MODEL, D_MODEL), lambda i: (0, 0)),
            pl.BlockSpec((PAIR_W, 1), lambda i: (0, 0)),
        ],
        out_specs=pl.BlockSpec((D_MODEL, TM), lambda i: (0, i)),
        out_shape=jax.ShapeDtypeStruct((D_MODEL, m), BF16),
        compiler_params=_params("parallel"),
        name="q_proj",
    )(h, g, w, gq_col)


def _kv_kernel(h_ref, g_ref, w_ref, gk_ref, klo_ref, khi_ref, ve_ref, vo_ref):
    kv = _dot(_rms(h_ref[...], g_ref[...]).astype(BF16), w_ref[...])
    tm = kv.shape[0]
    gk = gk_ref[...]
    zeros = jnp.zeros((tm, HEAD_DIM), F32)
    ones_t = jnp.ones((HEAD_DIM, tm), F32)
    klo, khi = [], []
    for i in range(N_KV_HEADS):
        k = _rms(kv[:, i * HEAD_DIM:(i + 1) * HEAD_DIM], gk)
        klo += [k, zeros]
        khi += [zeros, k]
    klo_ref[...] = jnp.concatenate(klo, axis=1).astype(BF16)
    khi_ref[...] = jnp.concatenate(khi, axis=1).astype(BF16)
    for i in range(N_KV_HEADS // 2):
        vt = kv[:, KV_W + i * PAIR_W:KV_W + (i + 1) * PAIR_W].T
        for hd in range(2):
            v_t = vt[hd * HEAD_DIM:(hd + 1) * HEAD_DIM]
            rows = slice((2 * i + hd) * PAIR_W, (2 * i + hd + 1) * PAIR_W)
            ve_ref[rows, :] = jnp.concatenate([v_t, ones_t], axis=0).astype(BF16)
            vo_ref[rows, :] = jnp.concatenate([ones_t, v_t], axis=0).astype(BF16)


def _shared_kv(h, g, w, gk):
    m = h.shape[0]
    wide = N_KV_HEADS * PAIR_W
    return pl.pallas_call(
        _kv_kernel,
        grid=(m // TM,),
        in_specs=[
            pl.BlockSpec((TM, D_MODEL), lambda i: (i, 0)),
            pl.BlockSpec((1, D_MODEL), lambda i: (0, 0)),
            pl.BlockSpec((D_MODEL, 2 * KV_W), lambda i: (0, 0)),
            pl.BlockSpec((1, HEAD_DIM), lambda i: (0, 0)),
        ],
        out_specs=[pl.BlockSpec((TM, wide), lambda i: (i, 0))] * 2
        + [pl.BlockSpec((wide, TM), lambda i: (0, i))] * 2,
        out_shape=[jax.ShapeDtypeStruct((m, wide), BF16)] * 2 + [jax.ShapeDtypeStruct((wide, m), BF16)] * 2,
        compiler_params=_params("parallel"),
        name="shared_kv",
    )(h, g, w, gk)


def _bias_kernel(rb_ref, sinks_ref, bucket_ref, o_ref):
    layer, later, kv, parity = (pl.program_id(a) for a in range(4))
    bucket = bucket_ref[...]
    sj = lax.broadcasted_iota(jnp.int32, bucket.shape, 0)
    qi = lax.broadcasted_iota(jnp.int32, bucket.shape, 1)
    dist = qi + BLOCK - sj
    kpos = later * BLOCK - BLOCK + sj
    mask = (dist >= 0) & (dist < WINDOW) & (kpos >= 0)
    for pair in range(GROUP_PAIRS):
        h = kv * GROUP + 2 * pair + parity
        acc = jnp.zeros(bucket.shape, F32)
        for b in range(N_BUCKETS):
            acc = jnp.where(bucket == b, rb_ref[b, h], acc)
        o_ref[:, pair * BLOCK:(pair + 1) * BLOCK] = jnp.where(
            sj == 0, sinks_ref[layer, h], jnp.where(mask, acc, -jnp.inf))


def _bias_table(rel_bias, sinks, bucket_t):
    n_layers = sinks.shape[0]
    smem = pl.BlockSpec(memory_space=pltpu.SMEM)
    return pl.pallas_call(
        _bias_kernel,
        grid=(n_layers, 2, N_KV_HEADS, 2),
        in_specs=[smem, smem, pl.BlockSpec((2 * BLOCK, BLOCK), lambda l, f, k, e: (0, 0))],
        out_specs=pl.BlockSpec((None, None, None, None, 2 * BLOCK, GROUP_PAIRS * BLOCK),
                               lambda l, f, k, e: (l, f, k, e, 0, 0)),
        out_shape=jax.ShapeDtypeStruct((n_layers, 2, N_KV_HEADS, 2, 2 * BLOCK, GROUP_PAIRS * BLOCK), F32),
        compiler_params=_params(*["parallel"] * 4),
        name="bias_table",
    )(rel_bias, sinks, bucket_t)


def _swa_kernel(q_ref, klp_ref, klc_ref, khp_ref, khc_ref, vep_ref, vec_ref, vop_ref, voc_ref,
                bias_ref, o_ref):
    k_keep = jnp.where(lax.broadcasted_iota(jnp.int32, (2 * BLOCK, PAIR_W), 0) == 0, 0.0, 1.0).astype(BF16)
    v_row = lax.broadcasted_iota(jnp.int32, (PAIR_W, 2 * BLOCK), 0)
    v_col = lax.broadcasted_iota(jnp.int32, (PAIR_W, 2 * BLOCK), 1)
    ve_keep = jnp.where((v_col == 0) & (v_row < HEAD_DIM), 0.0, 1.0).astype(BF16)
    vo_keep = jnp.where((v_col == 0) & (v_row >= HEAD_DIM), 0.0, 1.0).astype(BF16)

    for kv in range(N_KV_HEADS):
        tile = slice(kv * PAIR_W, (kv + 1) * PAIR_W)
        klo = jnp.concatenate([klp_ref[:, tile], klc_ref[:, tile]], axis=0) * k_keep
        khi = jnp.concatenate([khp_ref[:, tile], khc_ref[:, tile]], axis=0) * k_keep
        ve = jnp.concatenate([vep_ref[tile, :], vec_ref[tile, :]], axis=1) * ve_keep
        vo = jnp.concatenate([vop_ref[tile, :], voc_ref[tile, :]], axis=1) * vo_keep
        q0 = kv * GROUP_PAIRS * PAIR_W
        q_t = jnp.concatenate([q_ref[q0 + pr * PAIR_W:q0 + (pr + 1) * PAIR_W, :]
                               for pr in range(GROUP_PAIRS)], axis=1)

        s_both = _dot(jnp.concatenate([klo, khi], axis=0), q_t)

        def probs(parity):
            s = s_both[parity * 2 * BLOCK:(parity + 1) * 2 * BLOCK] + bias_ref[kv, parity]
            return jnp.exp(s - jnp.max(s, axis=0, keepdims=True)).astype(BF16)

        oe = _dot(ve, probs(0))
        oo = _dot(vo, probs(1))
        num = jnp.concatenate([oe[:HEAD_DIM], oo[HEAD_DIM:]], axis=0)
        den = jnp.concatenate([oe[HEAD_DIM:], oo[:HEAD_DIM]], axis=0)
        out_t = num / den
        for pr in range(GROUP_PAIRS):
            o_ref[:, q0 + pr * PAIR_W:q0 + (pr + 1) * PAIR_W] = (
                out_t[:, pr * BLOCK:(pr + 1) * BLOCK].T.astype(BF16))


def _swa_attn(q_t, klo, khi, ve_t, vo_t, bias, layer):
    bsz, seq, wide = klo.shape
    nb = seq // BLOCK
    prev = lambda b, n: (b, jnp.maximum(n - 1, 0), 0)
    cur = lambda b, n: (b, n, 0)
    prev_t = lambda b, n: (0, b * nb + jnp.maximum(n - 1, 0))
    cur_t = lambda b, n: (0, b * nb + n)
    k_prev, k_cur = pl.BlockSpec((None, BLOCK, wide), prev), pl.BlockSpec((None, BLOCK, wide), cur)
    v_prev, v_cur = pl.BlockSpec((wide, BLOCK), prev_t), pl.BlockSpec((wide, BLOCK), cur_t)
    return pl.pallas_call(
        _swa_kernel,
        grid=(bsz, nb),
        in_specs=[
            pl.BlockSpec((D_MODEL, BLOCK), cur_t),
            k_prev, k_cur, k_prev, k_cur, v_prev, v_cur, v_prev, v_cur,
            pl.BlockSpec((None, None, N_KV_HEADS, 2, 2 * BLOCK, GROUP_PAIRS * BLOCK),
                         lambda b, n: (layer, jnp.minimum(n, 1), 0, 0, 0, 0)),
        ],
        out_specs=pl.BlockSpec((None, BLOCK, D_MODEL), cur),
        out_shape=jax.ShapeDtypeStruct((bsz, seq, D_MODEL), BF16),
        compiler_params=_params("parallel", "arbitrary"),
        name="swa_attn",
    )(q_t, klo, klo, khi, khi, ve_t, ve_t, vo_t, vo_t, bias)


def _ple_kernel(h_ref, g_ref, wgate_ref, p_ref, wple_ref, o_ref):
    x = h_ref[...]
    gate = jax.nn.sigmoid(_dot(_rms(x, g_ref[...]).astype(BF16), wgate_ref[...]))
    o_ref[...] = x + gate * _dot(p_ref[...].astype(BF16), wple_ref[...])


def _ple(h, g, w_gate_all, p_all, w_ple, layer):
    m = h.shape[0]
    return pl.pallas_call(
        _ple_kernel,
        grid=(m // TM_WIDE,),
        in_specs=[
            pl.BlockSpec((TM_WIDE, D_MODEL), lambda i: (i, 0)),
            pl.BlockSpec((1, D_MODEL), lambda i: (0, 0)),
            pl.BlockSpec((None, D_MODEL, D_MODEL), lambda i: (layer, 0, 0), pipeline_mode=pl.Buffered(1)),
            pl.BlockSpec((None, TM_WIDE, PLE_DIM), lambda i: (layer, i, 0)),
            pl.BlockSpec((PLE_DIM, D_MODEL), lambda i: (0, 0), pipeline_mode=pl.Buffered(1)),
        ],
        out_specs=pl.BlockSpec((TM_WIDE, D_MODEL), lambda i: (i, 0)),
        out_shape=jax.ShapeDtypeStruct((m, D_MODEL), F32),
        compiler_params=_params("parallel"),
        name="ple",
    )(h, g, w_gate_all, p_all, w_ple)


def _t5_bucket(d):
    max_exact = N_BUCKETS // 2
    is_small = d < max_exact
    df = jnp.maximum(d, 1).astype(F32)
    large = max_exact + (jnp.log(df / max_exact) / math.log(MAX_DISTANCE / max_exact)
                         * (N_BUCKETS - max_exact)).astype(jnp.int32)
    large = jnp.minimum(large, N_BUCKETS - 1)
    return jnp.where(is_small, d, large)


def kernel(x, p, norm_g, ffn_wg, ffn_wu, ffn_wd, ple_w, ple_gate_w, gla_w_in, gla_w_a2, gla_b_a,
           gla_norm_g, gla_w_o, kv_norm_g, w_kv, k_norm_g, rel_bias, swa_w_q, q_norm_g, sinks, swa_w_o):
    bsz, seq, d = x.shape
    depth = norm_g.shape[0]
    n_a = gla_w_in.shape[0]
    m = bsz * seq
    row = lambda v: v.reshape(1, -1)

    sj = jnp.arange(2 * BLOCK)[:, None]
    qi = jnp.arange(BLOCK)[None, :]
    bucket_t = _t5_bucket(jnp.maximum(qi + BLOCK - sj, 0)).astype(jnp.int32)
    bias = _bias_table(rel_bias, sinks.astype(F32), bucket_t)

    h = x.reshape(m, d)
    kv_tiles = None
    p_all = p.reshape(depth, m, PLE_DIM)
    ffn_w = (ffn_wg[0, 0].astype(BF16), ffn_wu[0, 0].astype(BF16), ffn_wd[0, 0].astype(BF16))
    w_in_all = jnp.swapaxes(gla_w_in, 1, 2).astype(BF16)
    w_a1_all = jnp.pad(jnp.swapaxes(w_in_all[:, GLA_MAIN:, :], 1, 2),
                       ((0, 0), (0, 0), (0, V7X_LANES - GLA_RANK)))
    w_a2_all = jnp.pad(gla_w_a2, ((0, 0), (0, V7X_LANES - GLA_RANK), (0, 0))).astype(BF16)
    w_gate_all = None

    def ffn(h, g, layer, sub, extra_casts=()):
        nxt = (layer, sub + 1) if sub == 0 else (layer + 1, 0)
        next_weights = (ffn_wg, ffn_wu, ffn_wd) + nxt if nxt[0] < depth else None
        return _ffn(h, g, *ffn_w, next_weights=next_weights, extra_casts=extra_casts)

    for i in range(depth):
        if i == n_a:
            klo, khi, ve_t, vo_t = _shared_kv(h, row(kv_norm_g), w_kv.astype(BF16), row(k_norm_g))
            kv_tiles = (klo.reshape(bsz, seq, -1), khi.reshape(bsz, seq, -1), ve_t, vo_t)
        if i == 0:
            h, ffn_w, (w_gate_all,) = ffn(h, row(norm_g[i, 0]), i, 0, (ple_gate_w,))
        else:
            h, ffn_w, _ = ffn(h, row(norm_g[i, 0]), i, 0)
        if i < n_a:
            proj, log_a = _gla_proj(h, row(norm_g[i, 1]), w_in_all, i, w_a1_all[i], w_a2_all[i],
                                    row(gla_b_a[i]))
            og = _gla_rec(proj.reshape(bsz, seq, GLA_MAIN), log_a.reshape(bsz, seq, GLA_QK),
                          row(gla_norm_g[i]))
            h = _matmul_res(og.reshape(m, GLA_V), gla_w_o[i].astype(BF16), h)
        else:
            j = i - n_a
            gq_col = (jnp.tile(q_norm_g[j], 2) * (HEAD_DIM ** -0.5)).reshape(PAIR_W, 1)
            q_t = _q_proj(h, row(norm_g[i, 1]), swa_w_q[j].astype(BF16), gq_col)
            o = _swa_attn(q_t, *kv_tiles, bias, j)
            h = _matmul_res(o.reshape(m, d), swa_w_o[j].astype(BF16), h)
        h, ffn_w, _ = ffn(h, row(norm_g[i, 2]), i, 1)
        h = _ple(h, row(norm_g[i, 3]), w_gate_all, p_all, ple_w[i].astype(BF16), i)
    return h.reshape(bsz, seq, d)
```
